```python
import math
import jax, jax.numpy as jnp
from jax import lax
import numpy as np

D_MODEL = 2048
BATCH = 16
SEQ = 2048
DEPTH = 4
DEC_BATCH = 8
DEC_SEQ = 2048
PAST_LEN = 128

HEAD_DIM = 128
WIN_Q_HEADS = 8
WIN_KV_HEADS = 2
WIN_REP = WIN_Q_HEADS // WIN_KV_HEADS
WINDOW = 128
BLOCK = 128
DIFF_HEADS = 4
DIFF_HEAD_DIM = 128
N_ALIBI_HEADS = WIN_Q_HEADS + DIFF_HEADS
WIN_Q = WIN_Q_HEADS * HEAD_DIM
WIN_KV = WIN_KV_HEADS * HEAD_DIM
DIFF_QK = DIFF_HEADS * 2 * DIFF_HEAD_DIM
DIFF_V = DIFF_HEADS * 2 * DIFF_HEAD_DIM
SPLIT_SIZES = (WIN_Q, WIN_KV, WIN_KV, DIFF_QK, DIFF_QK, DIFF_V, D_MODEL, D_MODEL)
IN_COLS = WIN_Q + 2 * WIN_KV + 2 * DIFF_QK + DIFF_V + 2 * D_MODEL
D_FF = ((8 * D_MODEL // 3 + 127) // 128) * 128
N_EXPERTS = 8
TOP_K = 2
D_FF_EXPERT = D_MODEL // 2
N_DENSE = (DEPTH + 1) // 2
N_MOE = DEPTH // 2
ALPHA = (2.0 * DEPTH) ** 0.25
BETA = (8.0 * DEPTH) ** -0.25
LN_EPS = 1e-5
NEG_INF = -1e30

kernel_name = 'hybrid_window_diff_alibi_deepnorm_encoder'


def _layer_norm(x, g, b):
    xf = x.astype(jnp.float32)
    mu = jnp.mean(xf, axis=-1, keepdims=True)
    xc = xf - mu
    var = jnp.mean(xc * xc, axis=-1, keepdims=True)
    return (xc * lax.rsqrt(var + LN_EPS) * g.astype(jnp.float32) + b.astype(jnp.float32)).astype(x.dtype)


def _rms_norm(x, g):
    xf = x.astype(jnp.float32)
    return (xf * lax.rsqrt(jnp.mean(xf * xf, axis=-1, keepdims=True) + LN_EPS) * g.astype(jnp.float32)).astype(x.dtype)


def _alibi_slopes():
    h = jnp.arange(1, N_ALIBI_HEADS + 1, dtype=jnp.float32)
    return jnp.exp2(-8.0 * h / N_ALIBI_HEADS)


def _split_columns(z):
    parts, start = [], 0
    for size in SPLIT_SIZES:
        parts.append(z[..., start:start + size])
        start += size
    return parts


def _window_attention(q, k, v, sink, slopes):
    B, S = q.shape[0], q.shape[1]
    nb = S // BLOCK
    qb = q.reshape(B, nb, BLOCK, WIN_KV_HEADS, WIN_REP, HEAD_DIM)
    pad = ((0, 0), (BLOCK, BLOCK), (0, 0), (0, 0))
    kp = jnp.pad(k, pad).reshape(B, nb + 2, BLOCK, WIN_KV_HEADS, HEAD_DIM)
    vp = jnp.pad(v, pad).reshape(B, nb + 2, BLOCK, WIN_KV_HEADS, HEAD_DIM)
    kw = jnp.concatenate([kp[:, :-2], kp[:, 1:-1], kp[:, 2:]], axis=2)
    vw = jnp.concatenate([vp[:, :-2], vp[:, 1:-1], vp[:, 2:]], axis=2)
    s = jnp.einsum('bnqgrd,bnkgd->bngrqk', qb, kw).astype(jnp.float32) * (HEAD_DIM ** -0.5)
    dist = jnp.abs(jnp.arange(BLOCK)[:, None] - jnp.arange(3 * BLOCK)[None, :] + BLOCK)
    key_pos = (jnp.arange(nb)[:, None] - 1) * BLOCK + jnp.arange(3 * BLOCK)[None, :]
    valid = (dist[None] <= WINDOW) & (key_pos[:, None, :] >= 0) & (key_pos[:, None, :] < S)
    s = s - slopes.astype(jnp.float32)[:, :, None, None] * dist.astype(jnp.float32)
    s = jnp.where(valid[None, :, None, None], s, NEG_INF)
    sink_l = sink.astype(jnp.float32)[None, None, :, :, None]
    m = jnp.maximum(jnp.max(s, axis=-1), sink_l)
    p = jnp.exp(s - m[..., None])
    denom = jnp.sum(p, axis=-1) + jnp.exp(sink_l - m)
    p = p / denom[..., None]
    out = jnp.einsum('bngrqk,bnkgd->bnqgrd', p.astype(vw.dtype), vw)
    return out.reshape(B, S, WIN_Q)


def _diff_attention(q, k, v, lam, slopes):
    B, S = q.shape[0], q.shape[1]
    nb = S // BLOCK
    qb = jnp.moveaxis(q.reshape(B, nb, BLOCK, DIFF_HEADS, 2, DIFF_HEAD_DIM), 1, 0)
    key_pos = jnp.arange(S)
    sl = slopes.astype(jnp.float32)[:, None, None, None]

    def one_block(args):
        q_blk, n = args
        s = jnp.einsum('bqhcd,bkhcd->bhcqk', q_blk, k).astype(jnp.float32) * (DIFF_HEAD_DIM ** -0.5)
        dist = jnp.abs(n * BLOCK + jnp.arange(BLOCK)[:, None] - key_pos[None, :]).astype(jnp.float32)
        a = jax.nn.softmax(s - sl * dist, axis=-1)
        w = a[:, :, 0] - lam * a[:, :, 1]
        return jnp.einsum('bhqk,bkhe->bqhe', w.astype(v.dtype), v)

    out = lax.map(one_block, (qb, jnp.arange(nb)))
    return jnp.moveaxis(out, 0, 1).reshape(B, S, DIFF_HEADS, 2 * DIFF_HEAD_DIM)


def _mixer(x, w_in, sink, lq1, lk1, lq2, lk2, subln_g, w_pa, w_pd, w_o, lambda_init):
    B, S, _ = x.shape
    z = x @ w_in
    qa, ka, va, qd, kd, vd, ga, gd = _split_columns(z)
    slopes = _alibi_slopes()
    ya = _window_attention(
        qa.reshape(B, S, WIN_KV_HEADS, WIN_REP, HEAD_DIM),
        ka.reshape(B, S, WIN_KV_HEADS, HEAD_DIM),
        va.reshape(B, S, WIN_KV_HEADS, HEAD_DIM),
        sink.reshape(WIN_KV_HEADS, WIN_REP),
        slopes[:WIN_Q_HEADS].reshape(WIN_KV_HEADS, WIN_REP))
    lam = (jnp.exp(jnp.sum(lq1.astype(jnp.float32) * lk1.astype(jnp.float32)))
           - jnp.exp(jnp.sum(lq2.astype(jnp.float32) * lk2.astype(jnp.float32))) + lambda_init)
    yd = _diff_attention(
        qd.reshape(B, S, DIFF_HEADS, 2, DIFF_HEAD_DIM),
        kd.reshape(B, S, DIFF_HEADS, 2, DIFF_HEAD_DIM),
        vd.reshape(B, S, DIFF_HEADS, 2 * DIFF_HEAD_DIM),
        lam, slopes[WIN_Q_HEADS:])
    yd = (_rms_norm(yd, subln_g) * (1.0 - lambda_init)).reshape(B, S, DIFF_V)
    merged = jax.nn.sigmoid(ga) * (ya @ w_pa) + jax.nn.sigmoid(gd) * (yd @ w_pd)
    return merged @ w_o


def _swiglu(h, w1, w3, w2):
    return (jax.nn.silu(h @ w1) * (h @ w3)) @ w2


def _moe(h, w_router, w1, w3, w2):
    logits = jnp.einsum('bsd,de->bse', h, w_router).astype(jnp.float32)
    top_val, top_idx = lax.top_k(logits, TOP_K)
    gates = jax.nn.softmax(top_val, axis=-1)
    comb = jnp.sum(gates[..., None] * jax.nn.one_hot(top_idx, N_EXPERTS, dtype=jnp.float32), axis=-2)
    y = jnp.zeros_like(h)
    for e in range(N_EXPERTS):
        y = y + comb[..., e:e + 1].astype(h.dtype) * _swiglu(h, w1[e], w3[e], w2[e])
    return y


def setup_inputs(seed: int = 0) -> dict:
    key = jax.random.key(seed)
    ks = jax.random.split(key, 26)
    f32 = jnp.float32

    def nrm(k, shape, scale):
        return jax.random.normal(k, shape, f32) * scale

    col_scale = jnp.concatenate([
        jnp.ones((WIN_Q + WIN_KV,), f32), jnp.full((WIN_KV,), BETA, f32),
        jnp.ones((2 * DIFF_QK,), f32), jnp.full((DIFF_V,), BETA, f32),
        jnp.ones((2 * D_MODEL,), f32)])
    return {
        'x_prompt': nrm(ks[0], (BATCH, SEQ, D_MODEL), 1.0),
        'x_sample': nrm(ks[1], (DEC_BATCH, DEC_SEQ, D_MODEL), 1.0),
        'ln_emb_g': 1.0 + nrm(ks[2], (D_MODEL,), 0.02),
        'ln_emb_b': nrm(ks[3], (D_MODEL,), 0.02),
        'w_in': nrm(ks[4], (DEPTH, D_MODEL, IN_COLS), D_MODEL ** -0.5) * col_scale,
        'sink': nrm(ks[5], (DEPTH, WIN_Q_HEADS), 0.5),
        'lambda_q1': nrm(ks[6], (DEPTH, DIFF_HEAD_DIM), 0.1),
        'lambda_k1': nrm(ks[7], (DEPTH, DIFF_HEAD_DIM), 0.1),
        'lambda_q2': nrm(ks[8], (DEPTH, DIFF_HEAD_DIM), 0.1),
        'lambda_k2': nrm(ks[9], (DEPTH, DIFF_HEAD_DIM), 0.1),
        'subln_g': 1.0 + nrm(ks[10], (DEPTH, 2 * DIFF_HEAD_DIM), 0.02),
        'w_pa': nrm(ks[11], (DEPTH, WIN_Q, D_MODEL), WIN_Q ** -0.5 * BETA),
        'w_pd': nrm(ks[12], (DEPTH, DIFF_V, D_MODEL), DIFF_V ** -0.5 * BETA),
        'w_o': nrm(ks[13], (DEPTH, D_MODEL, D_MODEL), D_MODEL ** -0.5 * BETA),
        'ln1_g': 1.0 + nrm(ks[14], (DEPTH, D_MODEL), 0.02),
        'ln1_b': nrm(ks[15], (DEPTH, D_MODEL), 0.02),
        'ln2_g': 1.0 + nrm(ks[16], (DEPTH, D_MODEL), 0.02),
        'ln2_b': nrm(ks[17], (DEPTH, D_MODEL), 0.02),
        'ffn_w1': nrm(ks[18], (N_DENSE, D_MODEL, D_FF), D_MODEL ** -0.5 * BETA),
        'ffn_w3': nrm(ks[19], (N_DENSE, D_MODEL, D_FF), D_MODEL ** -0.5 * BETA),
        'ffn_w2': nrm(ks[20], (N_DENSE, D_FF, D_MODEL), D_FF ** -0.5 * BETA),
        'router_w': nrm(ks[21], (N_MOE, D_MODEL, N_EXPERTS), D_MODEL ** -0.5),
        'moe_w1': nrm(ks[22], (N_MOE, N_EXPERTS, D_MODEL, D_FF_EXPERT), D_MODEL ** -0.5 * BETA),
        'moe_w3': nrm(ks[23], (N_MOE, N_EXPERTS, D_MODEL, D_FF_EXPERT), D_MODEL ** -0.5 * BETA),
        'moe_w2': nrm(ks[24], (N_MOE, N_EXPERTS, D_FF_EXPERT, D_MODEL), D_FF_EXPERT ** -0.5 * BETA),
    }


def reference(x_prompt, x_sample, ln_emb_g, ln_emb_b, w_in, sink, lambda_q1, lambda_k1, lambda_q2, lambda_k2,
              subln_g, w_pa, w_pd, w_o, ln1_g, ln1_b, ln2_g, ln2_b, ffn_w1, ffn_w3, ffn_w2, router_w,
              moe_w1, moe_w3, moe_w2):
    def trunk(x):
        x = _layer_norm(x, ln_emb_g, ln_emb_b)
        for l in range(DEPTH):
            lambda_init = 0.8 - 0.6 * math.exp(-0.3 * l)
            mix = _mixer(x, w_in[l], sink[l], lambda_q1[l], lambda_k1[l], lambda_q2[l], lambda_k2[l],
                         subln_g[l], w_pa[l], w_pd[l], w_o[l], lambda_init)
            h = _layer_norm(ALPHA * x + mix, ln1_g[l], ln1_b[l])
            i = l // 2
            if l % 2 == 0:
                f = _swiglu(h, ffn_w1[i], ffn_w3[i], ffn_w2[i])
            else:
                f = _moe(h, router_w[i], moe_w1[i], moe_w3[i], moe_w2[i])
            x = _layer_norm(ALPHA * h + f, ln2_g[l], ln2_b[l])
        return x

    y_prompt = trunk(x_prompt)
    y_sample = trunk(x_sample)
    return (y_prompt, y_sample)
```

```python
import functools
import math

import jax
import jax.numpy as jnp
from jax import lax
from jax.experimental import pallas as pl
from jax.experimental.pallas import tpu as pltpu

HEAD_DIM = 128
WIN_Q_HEADS = 8
WIN_KV_HEADS = 2
WIN_REP = WIN_Q_HEADS // WIN_KV_HEADS
WINDOW = 128
BLOCK = 128
DIFF_HEADS = 4
N_ALIBI_HEADS = WIN_Q_HEADS + DIFF_HEADS
WIN_Q = WIN_Q_HEADS * HEAD_DIM
WIN_KV = WIN_KV_HEADS * HEAD_DIM
DIFF_QK = DIFF_HEADS * 2 * HEAD_DIM
DIFF_V = DIFF_HEADS * 2 * HEAD_DIM
QKV_COLS = WIN_Q + 2 * WIN_KV + 2 * DIFF_QK + DIFF_V
N_EXPERTS = 8
LN_EPS = 1e-5
NEG_INF = -1e30
LOG2E = 1.4426950408889634
ROUTER_LANES = 128

VMEM_LIMIT = 56 * 1024 * 1024
F32 = jnp.float32
BF16 = jnp.bfloat16


def _params(sem, vmem=VMEM_LIMIT):
    return pltpu.CompilerParams(dimension_semantics=sem, vmem_limit_bytes=vmem)


def _tile(n, want):
    t = min(n, want)
    assert n % t == 0, (n, t)
    return t


def _layer_norm(v, g, b):
    mu = jnp.mean(v, axis=-1, keepdims=True)
    xc = v - mu
    var = jnp.mean(xc * xc, axis=-1, keepdims=True)
    return xc * lax.rsqrt(var + LN_EPS) * g + b


def _row_tile_spec(tm, width):
    return pl.BlockSpec((tm, width), lambda i, *_: (i, 0), pipeline_mode=pl.Buffered(1))


def _dot(a, b):
    return jnp.dot(a, b, preferred_element_type=F32)


def _dot_t(a, b):
    return lax.dot_general(a, b, (((1,), (1,)), ((), ())), preferred_element_type=F32)


def _embed_ln_kernel(xp_ref, xs_ref, g_ref, b_ref, of_ref, ob_ref, *, n_prompt_tiles):
    i = pl.program_id(0)

    def emit(x_ref):
        y = _layer_norm(x_ref[...], g_ref[...], b_ref[...])
        of_ref[...] = y
        ob_ref[...] = y.astype(BF16)

    @pl.when(i < n_prompt_tiles)
    def _():
        emit(xp_ref)

    @pl.when(i >= n_prompt_tiles)
    def _():
        emit(xs_ref)


def _embed_ln(xp, xs, g, b):
    tp, d = xp.shape
    ts = xs.shape[0]
    tm = _tile(math.gcd(tp, ts), 512)
    npt, nst = tp // tm, ts // tm
    t = tp + ts
    return pl.pallas_call(
        functools.partial(_embed_ln_kernel, n_prompt_tiles=npt),
        grid=(npt + nst,),
        in_specs=[
            pl.BlockSpec((tm, d), lambda i: (jnp.minimum(i, npt - 1), 0)),
            pl.BlockSpec((tm, d), lambda i: (jnp.maximum(i - npt, 0), 0)),
            pl.BlockSpec((1, d), lambda i: (0, 0)),
            pl.BlockSpec((1, d), lambda i: (0, 0)),
        ],
        out_specs=[
            pl.BlockSpec((tm, d), lambda i: (i, 0)),
            pl.BlockSpec((tm, d), lambda i: (i, 0)),
        ],
        out_shape=[jax.ShapeDtypeStruct((t, d), F32), jax.ShapeDtypeStruct((t, d), BF16)],
        compiler_params=_params(("arbitrary",)),
        name="embed_ln",
    )(xp, xs, g.reshape(1, d), b.reshape(1, d))


def _inproj_kernel(x_ref, w_ref, o_ref):
    o_ref[...] = _dot(x_ref[...], w_ref[...]).astype(o_ref.dtype)


def _inproj(xb, w):
    t, d = xb.shape
    n = w.shape[1]
    tm = _tile(t, 1024)
    tn = _tile(n, 1536)
    return pl.pallas_call(
        _inproj_kernel,
        grid=(t // tm, n // tn),
        in_specs=[
            pl.BlockSpec((tm, d), lambda i, j: (i, 0)),
            pl.BlockSpec((d, tn), lambda i, j: (0, j)),
        ],
        out_specs=pl.BlockSpec((tm, tn), lambda i, j: (i, j)),
        out_shape=jax.ShapeDtypeStruct((t, n), BF16),
        compiler_params=_params(("parallel", "arbitrary")),
        name="inproj",
    )(xb, w)


def _win_attn_kernel(sc_ref, q_ref, k_ref, v_ref, o_ref, *, seq):
    g = pl.program_id(1)
    kw = min(3 * BLOCK, seq)
    scale = HEAD_DIM ** -0.5

    def body(n, carry):
        q0 = pl.multiple_of(n * BLOCK, BLOCK)
        k0 = pl.multiple_of(jnp.clip(n * BLOCK - BLOCK, 0, seq - kw), BLOCK)
        kblk = k_ref[pl.ds(k0, kw), :]
        vblk = v_ref[pl.ds(k0, kw), :]
        tpos = q0 + lax.broadcasted_iota(jnp.int32, (BLOCK, kw), 0)
        spos = k0 + lax.broadcasted_iota(jnp.int32, (BLOCK, kw), 1)
        dist = jnp.abs(tpos - spos)
        valid = dist <= WINDOW
        distf = dist.astype(F32)
        for r in range(WIN_REP):
            h = g * WIN_REP + r
            sink = sc_ref[h]
            slope = sc_ref[WIN_Q_HEADS + h]
            q = q_ref[pl.ds(q0, BLOCK), r * HEAD_DIM:(r + 1) * HEAD_DIM]
            s = _dot_t(q, kblk) * scale - slope * distf
            s = jnp.where(valid, s, NEG_INF)
            m = jnp.maximum(jnp.max(s, axis=-1, keepdims=True), sink)
            p = jnp.exp(s - m)
            denom = jnp.sum(p, axis=-1, keepdims=True) + jnp.exp(sink - m)
            o = _dot(p.astype(BF16), vblk) / denom
            o_ref[pl.ds(q0, BLOCK), r * HEAD_DIM:(r + 1) * HEAD_DIM] = o.astype(o_ref.dtype)
        return carry

    lax.fori_loop(0, seq // BLOCK, body, 0)


def _win_attn(z, scalars, n_seq, seq):
    gq = WIN_REP * HEAD_DIM
    k_blk0 = WIN_Q // HEAD_DIM
    v_blk0 = (WIN_Q + WIN_KV) // HEAD_DIM
    return pl.pallas_call(
        functools.partial(_win_attn_kernel, seq=seq),
        grid=(n_seq, WIN_KV_HEADS),
        in_specs=[
            pl.BlockSpec(memory_space=pltpu.SMEM),
            pl.BlockSpec((seq, gq), lambda b, g: (b, g)),
            pl.BlockSpec((seq, HEAD_DIM), lambda b, g: (b, k_blk0 + g)),
            pl.BlockSpec((seq, HEAD_DIM), lambda b, g: (b, v_blk0 + g)),
        ],
        out_specs=pl.BlockSpec((seq, gq), lambda b, g: (b, g)),
        out_shape=jax.ShapeDtypeStruct((n_seq * seq, WIN_Q), BF16),
        compiler_params=_params(("parallel", "arbitrary")),
        name="win_attn",
    )(scalars, z, z, z)


def _diff_attn_kernel(sl_ref, lq1_ref, lk1_ref, lq2_ref, lk2_ref, g_ref, q_ref, k_ref, v_ref, o_ref,
                      bias_ref, t_ref, *, seq, tq, kc, layer, lambda_init):
    h = pl.program_id(0)
    qi = pl.program_id(1)
    b = pl.program_id(2)
    n_kc = seq // kc
    c1 = (HEAD_DIM ** -0.5) * LOG2E

    @pl.when(b == 0)
    def _():
        slope2 = sl_ref[h] * LOG2E
        tpos = qi * tq + lax.broadcasted_iota(jnp.int32, (tq, seq), 0)
        spos = lax.broadcasted_iota(jnp.int32, (tq, seq), 1)
        bias_ref[...] = slope2 * jnp.abs(tpos - spos).astype(F32)

    lam = (jnp.exp(jnp.sum(lq1_ref[layer:layer + 1, :] * lk1_ref[layer:layer + 1, :], axis=-1, keepdims=True))
           - jnp.exp(jnp.sum(lq2_ref[layer:layer + 1, :] * lk2_ref[layer:layer + 1, :], axis=-1, keepdims=True))
           + lambda_init)

    outs = []
    for c in range(2):
        cs = slice(c * HEAD_DIM, (c + 1) * HEAD_DIM)
        q = q_ref[:, cs]
        m = jnp.full((tq, 1), -jnp.inf, F32)
        for j in range(n_kc):
            ks = slice(j * kc, (j + 1) * kc)
            t = _dot_t(q, k_ref[ks, cs]) * c1 - bias_ref[:, ks]
            t_ref[:, ks] = t
            m = jnp.maximum(m, jnp.max(t, axis=-1, keepdims=True))
        l = jnp.zeros((tq, 1), F32)
        acc = jnp.zeros((tq, 2 * HEAD_DIM), F32)
        for j in range(n_kc):
            ks = slice(j * kc, (j + 1) * kc)
            p = jnp.exp2(t_ref[:, ks] - m)
            l = l + jnp.sum(p, axis=-1, keepdims=True)
            acc = acc + _dot(p.astype(BF16), v_ref[ks, :])
        outs.append(acc / l)
    o = outs[0] - lam * outs[1]
    rms = lax.rsqrt(jnp.mean(o * o, axis=-1, keepdims=True) + LN_EPS)
    o_ref[...] = (o * rms * g_ref[layer:layer + 1, :] * (1.0 - lambda_init)).astype(o_ref.dtype)


def _diff_attn(z, slopes, lq1, lk1, lq2, lk2, subln_g, n_seq, seq, layer, lambda_init):
    hw = 2 * HEAD_DIM
    tq = _tile(seq, 512)
    kc = _tile(seq, 512)
    nq = seq // tq
    q_blk0 = (WIN_Q + 2 * WIN_KV) // hw
    k_blk0 = q_blk0 + DIFF_QK // hw
    v_blk0 = k_blk0 + DIFF_QK // hw
    whole = lambda a: pl.BlockSpec(a.shape, lambda h, qi, b: (0, 0))
    return pl.pallas_call(
        functools.partial(_diff_attn_kernel, seq=seq, tq=tq, kc=kc, layer=layer, lambda_init=lambda_init),
        grid=(DIFF_HEADS, nq, n_seq),
        in_specs=[
            pl.BlockSpec(memory_space=pltpu.SMEM),
            whole(lq1), whole(lk1), whole(lq2), whole(lk2), whole(subln_g),
            pl.BlockSpec((tq, hw), lambda h, qi, b: (b * nq + qi, q_blk0 + h)),
            pl.BlockSpec((seq, hw), lambda h, qi, b: (b, k_blk0 + h)),
            pl.BlockSpec((seq, hw), lambda h, qi, b: (b, v_blk0 + h)),
        ],
        out_specs=pl.BlockSpec((tq, hw), lambda h, qi, b: (b * nq + qi, h)),
        out_shape=jax.ShapeDtypeStruct((n_seq * seq, DIFF_V), BF16),
        scratch_shapes=[pltpu.VMEM((tq, seq), F32), pltpu.VMEM((tq, seq), F32)],
        compiler_params=_params(("arbitrary", "arbitrary", "arbitrary")),
        name="diff_attn",
    )(slopes, lq1, lk1, lq2, lk2, subln_g, z, z, z)


def _router_comb(hf, rhi_ref, rlo_ref):
    h_hi = hf.astype(BF16)
    h_lo = (hf - h_hi.astype(F32)).astype(BF16)
    logits = _dot(h_hi, rhi_ref[...]) + _dot(h_lo, rhi_ref[...]) + _dot(h_hi, rlo_ref[...])
    lane = lax.broadcasted_iota(jnp.int32, logits.shape, 1)
    lg = jnp.where(lane < N_EXPERTS, logits, -jnp.inf)
    m1 = jnp.max(lg, axis=-1, keepdims=True)
    i1 = jnp.min(jnp.where(lg == m1, lane, ROUTER_LANES), axis=-1, keepdims=True)
    lg2 = jnp.where(lane == i1, -jnp.inf, lg)
    m2 = jnp.max(lg2, axis=-1, keepdims=True)
    i2 = jnp.min(jnp.where(lg2 == m2, lane, ROUTER_LANES), axis=-1, keepdims=True)
    e2 = jnp.exp(m2 - m1)
    g1 = 1.0 / (1.0 + e2)
    g2 = e2 / (1.0 + e2)
    return jnp.where(lane == i1, g1, 0.0) + jnp.where(lane == i2, g2, 0.0)


def _merge_kernel(*refs, alpha, with_router):
    if with_router:
        (xb_ref, ya_ref, yd_ref, xf_ref, wga_ref, wgd_ref, wpa_ref, wpd_ref, wo_ref, g_ref, b_ref,
         rhi_ref, rlo_ref, hf_ref, hb_ref, comb_ref, acc_ref) = refs
    else:
        (xb_ref, ya_ref, yd_ref, xf_ref, wga_ref, wgd_ref, wpa_ref, wpd_ref, wo_ref, g_ref, b_ref,
         hf_ref, hb_ref, acc_ref) = refs
    j = pl.program_id(1)
    x = xb_ref[...]
    ga = _dot(x, wga_ref[...])
    gd = _dot(x, wgd_ref[...])
    pa = _dot(ya_ref[...], wpa_ref[...])
    pd = _dot(yd_ref[...], wpd_ref[...])
    merged = (jax.nn.sigmoid(ga) * pa + jax.nn.sigmoid(gd) * pd).astype(BF16)
    part = _dot(merged, wo_ref[...])

    @pl.when(j == 0)
    def _():
        acc_ref[...] = part

    @pl.when(j > 0)
    def _():
        acc_ref[...] += part

    @pl.when(j == pl.num_programs(1) - 1)
    def _():
        hf = _layer_norm(alpha * xf_ref[...] + acc_ref[...], g_ref[...], b_ref[...])
        hf_ref[...] = hf
        hb_ref[...] = hf.astype(BF16)
        if with_router:
            comb_ref[...] = _router_comb(hf, rhi_ref, rlo_ref)


def _merge(xb, ya, yd, xf, w_g, w_pa, w_pd, w_o, ln_g, ln_b, alpha, router=None):
    t, d = xb.shape
    tm = _tile(t, 512)
    tn = _tile(d, 512)
    nj = d // tn
    with_router = router is not None
    in_specs = [
        _row_tile_spec(tm, d),
        _row_tile_spec(tm, WIN_Q),
        _row_tile_spec(tm, DIFF_V),
        _row_tile_spec(tm, d),
        pl.BlockSpec((d, tn), lambda i, j: (0, j)),
        pl.BlockSpec((d, tn), lambda i, j: (0, nj + j)),
        pl.BlockSpec((WIN_Q, tn), lambda i, j: (0, j)),
        pl.BlockSpec((DIFF_V, tn), lambda i, j: (0, j)),
        pl.BlockSpec((tn, d), lambda i, j: (j, 0)),
        pl.BlockSpec((1, d), lambda i, j: (0, 0)),
        pl.BlockSpec((1, d), lambda i, j: (0, 0)),
    ]
    args = [xb, ya, yd, xf, w_g, w_g, w_pa, w_pd, w_o, ln_g.reshape(1, d), ln_b.reshape(1, d)]
    out_specs = [
        pl.BlockSpec((tm, d), lambda i, j: (i, 0)),
        pl.BlockSpec((tm, d), lambda i, j: (i, 0)),
    ]
    out_shape = [jax.ShapeDtypeStruct((t, d), F32), jax.ShapeDtypeStruct((t, d), BF16)]
    if with_router:
        in_specs += [pl.BlockSpec((d, ROUTER_LANES), lambda i, j: (0, 0))] * 2
        args += list(router)
        out_specs.append(pl.BlockSpec((tm, ROUTER_LANES), lambda i, j: (i, 0)))
        out_shape.append(jax.ShapeDtypeStruct((t, ROUTER_LANES), F32))
    return pl.pallas_call(
        functools.partial(_merge_kernel, alpha=alpha, with_router=with_router),
        grid=(t // tm, nj),
        in_specs=in_specs,
        out_specs=out_specs,
        out_shape=out_shape,
        scratch_shapes=[pltpu.VMEM((tm, d), F32)],
        compiler_params=_params(("parallel", "arbitrary")),
        name="merge_router" if with_router else "merge",
    )(*args)


def _ffn_step(hb_ref, w1_ref, w3_ref, w2_ref, acc_ref, first, scale=None):
    hb = hb_ref[...]
    u = _dot(hb, w1_ref[...])
    v = _dot(hb, w3_ref[...])
    gte = u * jax.nn.sigmoid(u) * v
    if scale is not None:
        gte = gte * scale
    part = _dot(gte.astype(BF16), w2_ref[...])

    @pl.when(first)
    def _():
        acc_ref[...] = part

    @pl.when(jnp.logical_not(first))
    def _():
        acc_ref[...] += part


def _ffn_finish(hf_ref, acc_ref, g_ref, b_ref, of_ref, ob_ref, alpha):
    y = _layer_norm(alpha * hf_ref[...] + acc_ref[...], g_ref[...], b_ref[...])
    of_ref[...] = y
    ob_ref[...] = y.astype(BF16)


def _ffn_kernel(hb_ref, hf_ref, w1_ref, w3_ref, w2_ref, g_ref, b_ref, of_ref, ob_ref, acc_ref, *, alpha):
    f = pl.program_id(1)
    _ffn_step(hb_ref, w1_ref, w3_ref, w2_ref, acc_ref, f == 0)

    @pl.when(f == pl.num_programs(1) - 1)
    def _():
        _ffn_finish(hf_ref, acc_ref, g_ref, b_ref, of_ref, ob_ref, alpha)


def _ffn(hb, hf, w1, w3, w2, ln_g, ln_b, alpha):
    t, d = hb.shape
    ff = w1.shape[1]
    tm = _tile(t, 512)
    tf = _tile(ff, 512)
    return pl.pallas_call(
        functools.partial(_ffn_kernel, alpha=alpha),
        grid=(t // tm, ff // tf),
        in_specs=[
            _row_tile_spec(tm, d),
            _row_tile_spec(tm, d),
            pl.BlockSpec((d, tf), lambda i, f: (0, f)),
            pl.BlockSpec((d, tf), lambda i, f: (0, f)),
            pl.BlockSpec((tf, d), lambda i, f: (f, 0)),
            pl.BlockSpec((1, d), lambda i, f: (0, 0)),
            pl.BlockSpec((1, d), lambda i, f: (0, 0)),
        ],
        out_specs=[
            pl.BlockSpec((tm, d), lambda i, f: (i, 0)),
            pl.BlockSpec((tm, d), lambda i, f: (i, 0)),
        ],
        out_shape=[jax.ShapeDtypeStruct((t, d), F32), jax.ShapeDtypeStruct((t, d), BF16)],
        scratch_shapes=[pltpu.VMEM((tm, d), F32)],
        compiler_params=_params(("parallel", "arbitrary")),
        name="ffn",
    )(hb, hf, w1, w3, w2, ln_g.reshape(1, d), ln_b.reshape(1, d))


def _moe_kernel(hb_ref, hf_ref, comb_ref, w1_ref, w3_ref, w2_ref, g_ref, b_ref, of_ref, ob_ref, acc_ref, *, alpha):
    e = pl.program_id(1)
    f = pl.program_id(2)
    comb = comb_ref[...]
    lane = lax.broadcasted_iota(jnp.int32, comb.shape, 1)
    scale = jnp.sum(jnp.where(lane == e, comb, 0.0), axis=-1, keepdims=True)
    _ffn_step(hb_ref, w1_ref.at[0], w3_ref.at[0], w2_ref.at[0], acc_ref, jnp.logical_and(e == 0, f == 0), scale)

    @pl.when(jnp.logical_and(e == pl.num_programs(1) - 1, f == pl.num_programs(2) - 1))
    def _():
        _ffn_finish(hf_ref, acc_ref, g_ref, b_ref, of_ref, ob_ref, alpha)


def _moe(hb, hf, comb, w1, w3, w2, ln_g, ln_b, alpha):
    t, d = hb.shape
    ne, _, ff = w1.shape
    tm = _tile(t, 512)
    tf = _tile(ff, 512)
    return pl.pallas_call(
        functools.partial(_moe_kernel, alpha=alpha),
        grid=(t // tm, ne, ff // tf),
        in_specs=[
            _row_tile_spec(tm, d),
            _row_tile_spec(tm, d),
            _row_tile_spec(tm, ROUTER_LANES),
            pl.BlockSpec((1, d, tf), lambda i, e, f: (e, 0, f)),
            pl.BlockSpec((1, d, tf), lambda i, e, f: (e, 0, f)),
            pl.BlockSpec((1, tf, d), lambda i, e, f: (e, f, 0)),
            pl.BlockSpec((1, d), lambda i, e, f: (0, 0)),
            pl.BlockSpec((1, d), lambda i, e, f: (0, 0)),
        ],
        out_specs=[
            pl.BlockSpec((tm, d), lambda i, e, f: (i, 0)),
            pl.BlockSpec((tm, d), lambda i, e, f: (i, 0)),
        ],
        out_shape=[jax.ShapeDtypeStruct((t, d), F32), jax.ShapeDtypeStruct((t, d), BF16)],
        scratch_shapes=[pltpu.VMEM((tm, d), F32)],
        compiler_params=_params(("parallel", "arbitrary", "arbitrary")),
        name="moe",
    )(hb, hf, comb, w1, w3, w2, ln_g.reshape(1, d), ln_b.reshape(1, d))


def _pad_to(a, axis, mult):
    pad = (-a.shape[axis]) % mult
    if pad == 0:
        return a
    widths = [(0, 0)] * a.ndim
    widths[axis] = (0, pad)
    return jnp.pad(a, widths)


def kernel(x_prompt, x_sample, ln_emb_g, ln_emb_b, w_in, sink, lambda_q1, lambda_k1, lambda_q2, lambda_k2,
           subln_g, w_pa, w_pd, w_o, ln1_g, ln1_b, ln2_g, ln2_b, ffn_w1, ffn_w3, ffn_w2, router_w,
           moe_w1, moe_w3, moe_w2):
    bp, seq, d = x_prompt.shape
    bs, seq_s, _ = x_sample.shape
    assert seq == seq_s
    depth = w_in.shape[0]
    n_seq = bp + bs
    alpha = (2.0 * depth) ** 0.25
    slopes = [2.0 ** (-8.0 * (h + 1) / N_ALIBI_HEADS) for h in range(N_ALIBI_HEADS)]
    diff_slopes = jnp.asarray(slopes[WIN_Q_HEADS:], F32)
    win_slopes = jnp.asarray(slopes[:WIN_Q_HEADS], F32)

    xf, xb = _embed_ln(x_prompt.reshape(bp * seq, d), x_sample.reshape(bs * seq, d), ln_emb_g, ln_emb_b)

    for l in range(depth):
        lambda_init = 0.8 - 0.6 * math.exp(-0.3 * l)
        w_qkv = w_in[l, :, :QKV_COLS].astype(BF16)
        w_g = w_in[l, :, QKV_COLS:].astype(BF16)
        z = _inproj(xb, w_qkv)
        ya = _win_attn(z, jnp.concatenate([sink[l].astype(F32), win_slopes]), n_seq, seq)
        yd = _diff_attn(z, diff_slopes, lambda_q1, lambda_k1, lambda_q2, lambda_k2, subln_g,
                        n_seq, seq, l, lambda_init)
        i = l // 2
        if l % 2 == 0:
            hf, hb = _merge(xb, ya, yd, xf, w_g, w_pa[l].astype(BF16), w_pd[l].astype(BF16),
                            w_o[l].astype(BF16), ln1_g[l], ln1_b[l], alpha)
            w1 = _pad_to(ffn_w1[i], 1, 512).astype(BF16)
            w3 = _pad_to(ffn_w3[i], 1, 512).astype(BF16)
            w2 = _pad_to(ffn_w2[i], 0, 512).astype(BF16)
            xf, xb = _ffn(hb, hf, w1, w3, w2, ln2_g[l], ln2_b[l], alpha)
        else:
            rw = _pad_to(router_w[i].astype(F32), 1, ROUTER_LANES)
            r_hi = rw.astype(BF16)
            r_lo = (rw - r_hi.astype(F32)).astype(BF16)
            hf, hb, comb = _merge(xb, ya, yd, xf, w_g, w_pa[l].astype(BF16), w_pd[l].astype(BF16),
                                  w_o[l].astype(BF16), ln1_g[l], ln1_b[l], alpha, router=(r_hi, r_lo))
            xf, xb = _moe(hb, hf, comb, moe_w1[i].astype(BF16), moe_w3[i].astype(BF16),
                          moe_w2[i].astype(BF16), ln2_g[l], ln2_b[l], alpha)

    y = xf.reshape(n_seq, seq, d)
    return (y[:bp], y[bp:])
```

```python
import functools
import math

import jax
import jax.numpy as jnp
from jax import lax
from jax.experimental import pallas as pl
from jax.experimental.pallas import tpu as pltpu

HEAD_DIM = 128
WIN_Q_HEADS = 8
WIN_KV_HEADS = 2
WIN_REP = WIN_Q_HEADS // WIN_KV_HEADS
WINDOW = 128
BLOCK = 128
DIFF_HEADS = 4
N_ALIBI_HEADS = WIN_Q_HEADS + DIFF_HEADS
WIN_Q = WIN_Q_HEADS * HEAD_DIM
WIN_KV = WIN_KV_HEADS * HEAD_DIM
DIFF_QK = DIFF_HEADS * 2 * HEAD_DIM
DIFF_V = DIFF_HEADS * 2 * HEAD_DIM
QKV_COLS = WIN_Q + 2 * WIN_KV + 2 * DIFF_QK + DIFF_V
DIFF_Q_COL0 = WIN_Q + 2 * WIN_KV
N_EXPERTS = 8
LN_EPS = 1e-5
NEG_INF = -1e30
LOG2E = 1.4426950408889634
Q_SCALE = (HEAD_DIM ** -0.5) * LOG2E
LANES = 128
R_E1, R_E2, R_G1, R_G2, R_RANK1, R_RANK2 = range(6)

VMEM_LIMIT = 56 * 1024 * 1024
F32 = jnp.float32
BF16 = jnp.bfloat16
I32 = jnp.int32

ROW_TILE = 512
FFN_ROW_TILE = 512
COMBINE_TILE = 256


def _params(sem, vmem=VMEM_LIMIT):
    return pltpu.CompilerParams(dimension_semantics=sem, vmem_limit_bytes=vmem)


def _tile(n, want):
    t = min(n, want)
    assert n % t == 0, (n, t)
    return t


def _layer_norm(v, g, b):
    mu = jnp.mean(v, axis=-1, keepdims=True)
    xc = v - mu
    var = jnp.mean(xc * xc, axis=-1, keepdims=True)
    return xc * lax.rsqrt(var + LN_EPS) * g + b


def _dot(a, b):
    return jnp.dot(a, b, preferred_element_type=F32)


def _dot_t(a, b):
    return lax.dot_general(a, b, (((1,), (1,)), ((), ())), preferred_element_type=F32)


def _embed_ln_kernel(xp_ref, xs_ref, g_ref, b_ref, of_ref, ob_ref, *, n_prompt_tiles):
    i = pl.program_id(0)

    def emit(x_ref):
        y = _layer_norm(x_ref[...], g_ref[...], b_ref[...])
        of_ref[...] = y
        ob_ref[...] = y.astype(BF16)

    @pl.when(i < n_prompt_tiles)
    def _():
        emit(xp_ref)

    @pl.when(i >= n_prompt_tiles)
    def _():
        emit(xs_ref)


def _embed_ln(xp, xs, g, b):
    tp, d = xp.shape
    ts = xs.shape[0]
    tm = _tile(math.gcd(tp, ts), 512)
    npt, nst = tp // tm, ts // tm
    t = tp + ts
    return pl.pallas_call(
        functools.partial(_embed_ln_kernel, n_prompt_tiles=npt),
        grid=(npt + nst,),
        in_specs=[
            pl.BlockSpec((tm, d), lambda i: (jnp.minimum(i, npt - 1), 0)),
            pl.BlockSpec((tm, d), lambda i: (jnp.maximum(i - npt, 0), 0)),
            pl.BlockSpec((1, d), lambda i: (0, 0)),
            pl.BlockSpec((1, d), lambda i: (0, 0)),
        ],
        out_specs=[
            pl.BlockSpec((tm, d), lambda i: (i, 0)),
            pl.BlockSpec((tm, d), lambda i: (i, 0)),
        ],
        out_shape=[jax.ShapeDtypeStruct((t, d), F32), jax.ShapeDtypeStruct((t, d), BF16)],
        compiler_params=_params(("arbitrary",)),
        name="embed_ln",
    )(xp, xs, g.reshape(1, d), b.reshape(1, d))


def _inproj_kernel(x_ref, w_ref, s_ref, o_ref):
    o_ref[...] = (_dot(x_ref[...], w_ref[...]) * s_ref[...]).astype(o_ref.dtype)


def _inproj(xb, w, col_scale):
    t, d = xb.shape
    n = w.shape[1]
    tm = _tile(t, 1024)
    tn = _tile(n, 1536)
    return pl.pallas_call(
        _inproj_kernel,
        grid=(t // tm, n // tn),
        in_specs=[
            pl.BlockSpec((tm, d), lambda i, j: (i, 0)),
            pl.BlockSpec((d, tn), lambda i, j: (0, j)),
            pl.BlockSpec((1, tn), lambda i, j: (0, j)),
        ],
        out_specs=pl.BlockSpec((tm, tn), lambda i, j: (i, j)),
        out_shape=jax.ShapeDtypeStruct((t, n), BF16),
        compiler_params=_params(("parallel", "arbitrary")),
        name="inproj",
    )(xb, w, col_scale)


def _win_attn_kernel(sc_ref, q_ref, k_ref, v_ref, o_ref, *, seq):
    g = pl.program_id(1)
    kw = min(3 * BLOCK, seq)
    row = lax.broadcasted_iota(I32, (BLOCK, kw), 0)
    col = lax.broadcasted_iota(I32, (BLOCK, kw), 1)

    def body(n, carry):
        q0 = pl.multiple_of(n * BLOCK, BLOCK)
        k0 = pl.multiple_of(jnp.clip(n * BLOCK - BLOCK, 0, seq - kw), BLOCK)
        kblk = k_ref[pl.ds(k0, kw), :]
        vblk = v_ref[pl.ds(k0, kw), :]
        dist = jnp.abs(row - col + (q0 - k0))
        valid = dist <= WINDOW
        distf = dist.astype(F32)
        for r in range(WIN_REP):
            h = g * WIN_REP + r
            sink2 = sc_ref[h]
            slope2 = sc_ref[WIN_Q_HEADS + h]
            cs = slice(r * HEAD_DIM, (r + 1) * HEAD_DIM)
            s = _dot_t(q_ref[pl.ds(q0, BLOCK), cs], kblk) - slope2 * distf
            s = jnp.where(valid, s, NEG_INF)
            m = jnp.maximum(jnp.max(s, axis=-1, keepdims=True), sink2)
            p = jnp.exp2(s - m)
            denom = jnp.sum(p, axis=-1, keepdims=True) + jnp.exp2(sink2 - m)
            o_ref[pl.ds(q0, BLOCK), cs] = (_dot(p.astype(BF16), vblk) / denom).astype(o_ref.dtype)
        return carry

    lax.fori_loop(0, seq // BLOCK, body, 0)


def _win_attn(z, scalars, n_seq, seq):
    gq = WIN_REP * HEAD_DIM
    k_blk0 = WIN_Q // HEAD_DIM
    v_blk0 = (WIN_Q + WIN_KV) // HEAD_DIM
    return pl.pallas_call(
        functools.partial(_win_attn_kernel, seq=seq),
        grid=(n_seq, WIN_KV_HEADS),
        in_specs=[
            pl.BlockSpec(memory_space=pltpu.SMEM),
            pl.BlockSpec((seq, gq), lambda b, g: (b, g)),
            pl.BlockSpec((seq, HEAD_DIM), lambda b, g: (b, k_blk0 + g)),
            pl.BlockSpec((seq, HEAD_DIM), lambda b, g: (b, v_blk0 + g)),
        ],
        out_specs=pl.BlockSpec((seq, gq), lambda b, g: (b, g)),
        out_shape=jax.ShapeDtypeStruct((n_seq * seq, WIN_Q), BF16),
        compiler_params=_params(("parallel", "arbitrary")),
        name="win_attn",
    )(scalars, z, z, z)


def _diff_attn_kernel(sl_ref, lq1_ref, lk1_ref, lq2_ref, lk2_ref, g_ref, q_ref, k_ref, v_ref, o_ref,
                      bias_ref, t_ref, *, seq, tq, kc, layer, lambda_init):
    h = pl.program_id(0)
    qi = pl.program_id(1)
    b = pl.program_id(2)
    n_kc = seq // kc

    @pl.when(b == 0)
    def _():
        tpos = qi * tq + lax.broadcasted_iota(I32, (tq, seq), 0)
        spos = lax.broadcasted_iota(I32, (tq, seq), 1)
        bias_ref[...] = sl_ref[h] * jnp.abs(tpos - spos).astype(F32)

    lam = (jnp.exp(jnp.sum(lq1_ref[layer:layer + 1, :] * lk1_ref[layer:layer + 1, :], axis=-1, keepdims=True))
           - jnp.exp(jnp.sum(lq2_ref[layer:layer + 1, :] * lk2_ref[layer:layer + 1, :], axis=-1, keepdims=True))
           + lambda_init)

    outs = []
    for c in range(2):
        cs = slice(c * HEAD_DIM, (c + 1) * HEAD_DIM)
        q = q_ref[:, cs]
        m = jnp.full((tq, 1), -jnp.inf, F32)
        for j in range(n_kc):
            ks = slice(j * kc, (j + 1) * kc)
            t = _dot_t(q, k_ref[ks, cs]) - bias_ref[:, ks]
            t_ref[:, ks] = t
            m = jnp.maximum(m, jnp.max(t, axis=-1, keepdims=True))
        l = jnp.zeros((tq, 1), F32)
        acc = jnp.zeros((tq, 2 * HEAD_DIM), F32)
        for j in range(n_kc):
            ks = slice(j * kc, (j + 1) * kc)
            p = jnp.exp2(t_ref[:, ks] - m)
            l = l + jnp.sum(p, axis=-1, keepdims=True)
            acc = acc + _dot(p.astype(BF16), v_ref[ks, :])
        outs.append(acc / l)
    o = outs[0] - lam * outs[1]
    rms = lax.rsqrt(jnp.mean(o * o, axis=-1, keepdims=True) + LN_EPS)
    o_ref[...] = (o * rms * g_ref[layer:layer + 1, :] * (1.0 - lambda_init)).astype(o_ref.dtype)


def _diff_attn(z, slopes2, lq1, lk1, lq2, lk2, subln_g, n_seq, seq, layer, lambda_init):
    hw = 2 * HEAD_DIM
    tq = _tile(seq, 512)
    kc = _tile(seq, 512)
    nq = seq // tq
    q_blk0 = DIFF_Q_COL0 // hw
    k_blk0 = q_blk0 + DIFF_QK // hw
    v_blk0 = k_blk0 + DIFF_QK // hw
    whole = lambda a: pl.BlockSpec(a.shape, lambda h, qi, b: (0, 0))
    return pl.pallas_call(
        functools.partial(_diff_attn_kernel, seq=seq, tq=tq, kc=kc, layer=layer, lambda_init=lambda_init),
        grid=(DIFF_HEADS, nq, n_seq),
        in_specs=[
            pl.BlockSpec(memory_space=pltpu.SMEM),
            whole(lq1), whole(lk1), whole(lq2), whole(lk2), whole(subln_g),
            pl.BlockSpec((tq, hw), lambda h, qi, b: (b * nq + qi, q_blk0 + h)),
            pl.BlockSpec((seq, hw), lambda h, qi, b: (b, k_blk0 + h)),
            pl.BlockSpec((seq, hw), lambda h, qi, b: (b, v_blk0 + h)),
        ],
        out_specs=pl.BlockSpec((tq, hw), lambda h, qi, b: (b * nq + qi, h)),
        out_shape=jax.ShapeDtypeStruct((n_seq * seq, DIFF_V), BF16),
        scratch_shapes=[pltpu.VMEM((tq, seq), F32), pltpu.VMEM((tq, seq), F32)],
        compiler_params=_params(("arbitrary", "arbitrary", "arbitrary")),
        name="diff_attn",
    )(slopes2, lq1, lk1, lq2, lk2, subln_g, z, z, z)


def _residual_copy(x_hbm, buf, sem, tm):
    row0 = pl.multiple_of(pl.program_id(0) * tm, tm)
    return pltpu.make_async_copy(x_hbm.at[pl.ds(row0, tm), :], buf, sem)


def _route(hf, rhi_ref, rlo_ref):
    tm = hf.shape[0]
    h_hi = hf.astype(BF16)
    h_lo = (hf - h_hi.astype(F32)).astype(BF16)
    logits = _dot(h_hi, rhi_ref[...]) + _dot(h_lo, rhi_ref[...]) + _dot(h_hi, rlo_ref[...])
    lane = lax.broadcasted_iota(I32, logits.shape, 1)
    lg = jnp.where(lane < N_EXPERTS, logits, -jnp.inf)
    m1 = jnp.max(lg, axis=-1, keepdims=True)
    i1 = jnp.min(jnp.where(lg == m1, lane, LANES), axis=-1, keepdims=True)
    lg2 = jnp.where(lane == i1, -jnp.inf, lg)
    m2 = jnp.max(lg2, axis=-1, keepdims=True)
    i2 = jnp.min(jnp.where(lg2 == m2, lane, LANES), axis=-1, keepdims=True)
    e2 = jnp.exp(m2 - m1)
    g1 = 1.0 / (1.0 + e2)
    g2 = e2 / (1.0 + e2)
    oh1 = lane == i1
    oh2 = lane == i2
    sel = jnp.logical_or(oh1, oh2).astype(F32)
    r_idx = lax.broadcasted_iota(I32, (tm, tm), 0)
    c_idx = lax.broadcasted_iota(I32, (tm, tm), 1)
    lower = (c_idx < r_idx).astype(F32).astype(BF16)
    before = _dot(lower, sel.astype(BF16))
    rank1 = jnp.sum(jnp.where(oh1, before, 0.0), axis=-1, keepdims=True)
    rank2 = jnp.sum(jnp.where(oh2, before, 0.0), axis=-1, keepdims=True)
    counts = jnp.sum(sel, axis=0, keepdims=True)
    rec = jnp.zeros(logits.shape, F32)
    for ln, val in ((R_E1, i1.astype(F32)), (R_E2, i2.astype(F32)), (R_G1, g1), (R_G2, g2),
                    (R_RANK1, rank1), (R_RANK2, rank2)):
        rec = jnp.where(lane == ln, val, rec)
    return rec, counts


def _merge_kernel(*refs, alpha, with_router, tm):
    if with_router:
        (xb_ref, ya_ref, yd_ref, xf_hbm, wga_ref, wgd_ref, wpa_ref, wpd_ref, wo_ref, g_ref, b_ref,
         rhi_ref, rlo_ref, hf_ref, hb_ref, rec_ref, cnt_ref, xf_buf, xf_sem) = refs
    else:
        (xb_ref, ya_ref, yd_ref, xf_hbm, wga_ref, wgd_ref, wpa_ref, wpd_ref, wo_ref, g_ref, b_ref,
         hf_ref, hb_ref, xf_buf, xf_sem) = refs
    j = pl.program_id(1)

    @pl.when(j == 0)
    def _():
        _residual_copy(xf_hbm, xf_buf, xf_sem, tm).start()
        hf_ref[...] = jnp.zeros(hf_ref.shape, F32)

    x = xb_ref[...]
    ga = _dot(x, wga_ref[...])
    gd = _dot(x, wgd_ref[...])
    pa = _dot(ya_ref[...], wpa_ref[...])
    pd = _dot(yd_ref[...], wpd_ref[...])
    merged = (jax.nn.sigmoid(ga) * pa + jax.nn.sigmoid(gd) * pd).astype(BF16)
    hf_ref[...] += _dot(merged, wo_ref[...])

    @pl.when(j == pl.num_programs(1) - 1)
    def _():
        _residual_copy(xf_hbm, xf_buf, xf_sem, tm).wait()
        hf = _layer_norm(alpha * xf_buf[...] + hf_ref[...], g_ref[...], b_ref[...])
        hf_ref[...] = hf
        hb_ref[...] = hf.astype(BF16)
        if with_router:
            rec, counts = _route(hf, rhi_ref, rlo_ref)
            rec_ref[...] = rec
            cnt_ref[...] = jnp.broadcast_to(counts, cnt_ref.shape)


def _merge(xb, ya, yd, xf, w_g, w_pa, w_pd, w_o, ln_g, ln_b, alpha, router=None):
    t, d = xb.shape
    tm = _tile(t, ROW_TILE)
    tn = _tile(d, 512)
    nj = d // tn
    with_router = router is not None
    row = lambda w: pl.BlockSpec((tm, w), lambda i, j: (i, 0))
    in_specs = [
        row(d), row(WIN_Q), row(DIFF_V),
        pl.BlockSpec(memory_space=pl.ANY),
        pl.BlockSpec((d, tn), lambda i, j: (0, j)),
        pl.BlockSpec((d, tn), lambda i, j: (0, nj + j)),
        pl.BlockSpec((WIN_Q, tn), lambda i, j: (0, j)),
        pl.BlockSpec((DIFF_V, tn), lambda i, j: (0, j)),
        pl.BlockSpec((tn, d), lambda i, j: (j, 0)),
        pl.BlockSpec((1, d), lambda i, j: (0, 0)),
        pl.BlockSpec((1, d), lambda i, j: (0, 0)),
    ]
    args = [xb, ya, yd, xf, w_g, w_g, w_pa, w_pd, w_o, ln_g.reshape(1, d), ln_b.reshape(1, d)]
    out_specs = [row(d), row(d)]
    out_shape = [jax.ShapeDtypeStruct((t, d), F32), jax.ShapeDtypeStruct((t, d), BF16)]
    if with_router:
        in_specs += [pl.BlockSpec((d, LANES), lambda i, j: (0, 0))] * 2
        args += list(router)
        out_specs += [row(LANES), pl.BlockSpec((8, LANES), lambda i, j: (i, 0))]
        out_shape += [jax.ShapeDtypeStruct((t, LANES), F32), jax.ShapeDtypeStruct((t // tm * 8, LANES), F32)]
    return pl.pallas_call(
        functools.partial(_merge_kernel, alpha=alpha, with_router=with_router, tm=tm),
        grid=(t // tm, nj),
        in_specs=in_specs,
        out_specs=out_specs,
        out_shape=out_shape,
        scratch_shapes=[pltpu.VMEM((tm, d), F32), pltpu.SemaphoreType.DMA(())],
        compiler_params=_params(("arbitrary", "arbitrary")),
        name="merge_router" if with_router else "merge",
    )(*args)


def _swiglu(x, w1, w3):
    u = _dot(x, w1)
    return (u * jax.nn.sigmoid(u) * _dot(x, w3)).astype(BF16)


def _ffn_kernel(hb_ref, hf_hbm, w1_ref, w3_ref, w2_ref, g_ref, b_ref, of_ref, ob_ref, hf_buf, hf_sem, *, alpha, tm):
    f = pl.program_id(1)

    @pl.when(f == 0)
    def _():
        _residual_copy(hf_hbm, hf_buf, hf_sem, tm).start()
        of_ref[...] = jnp.zeros(of_ref.shape, F32)

    of_ref[...] += _dot(_swiglu(hb_ref[...], w1_ref[...], w3_ref[...]), w2_ref[...])

    @pl.when(f == pl.num_programs(1) - 1)
    def _():
        _residual_copy(hf_hbm, hf_buf, hf_sem, tm).wait()
        y = _layer_norm(alpha * hf_buf[...] + of_ref[...], g_ref[...], b_ref[...])
        of_ref[...] = y
        ob_ref[...] = y.astype(BF16)


def _ffn(hb, hf, w1, w3, w2, ln_g, ln_b, alpha):
    t, d = hb.shape
    ff = w1.shape[1]
    tm = _tile(t, FFN_ROW_TILE)
    tf = _tile(ff, 512)
    row = pl.BlockSpec((tm, d), lambda i, f: (i, 0))
    return pl.pallas_call(
        functools.partial(_ffn_kernel, alpha=alpha, tm=tm),
        grid=(t // tm, ff // tf),
        in_specs=[
            row,
            pl.BlockSpec(memory_space=pl.ANY),
            pl.BlockSpec((d, tf), lambda i, f: (0, f)),
            pl.BlockSpec((d, tf), lambda i, f: (0, f)),
            pl.BlockSpec((tf, d), lambda i, f: (f, 0)),
            pl.BlockSpec((1, d), lambda i, f: (0, 0)),
            pl.BlockSpec((1, d), lambda i, f: (0, 0)),
        ],
        out_specs=[row, row],
        out_shape=[jax.ShapeDtypeStruct((t, d), F32), jax.ShapeDtypeStruct((t, d), BF16)],
        scratch_shapes=[pltpu.VMEM((tm, d), F32), pltpu.SemaphoreType.DMA(())],
        compiler_params=_params(("arbitrary", "arbitrary")),
        name="ffn",
    )(hb, hf, w1, w3, w2, ln_g.reshape(1, d), ln_b.reshape(1, d))


def _moe_plan(rec, cnt, tm_route, tmx):
    t = rec.shape[0]
    e1 = rec[:, R_E1].astype(I32)
    e2 = rec[:, R_E2].astype(I32)
    rank1 = rec[:, R_RANK1].astype(I32)
    rank2 = rec[:, R_RANK2].astype(I32)
    tile_cnt = cnt.reshape(t // tm_route, 8, LANES)[:, 0, :N_EXPERTS].astype(I32)
    counts = jnp.sum(tile_cnt, axis=0)
    tile_base = jnp.cumsum(tile_cnt, axis=0) - tile_cnt
    pad_cnt = (counts + tmx - 1) // tmx * tmx
    pad_end = jnp.cumsum(pad_cnt)
    pad_start = pad_end - pad_cnt
    base = jnp.repeat(tile_base + pad_start[None, :], tm_route, axis=0)
    eids = jnp.arange(N_EXPERTS, dtype=I32)[None, :]
    pos1 = jnp.sum(jnp.where(e1[:, None] == eids, base, 0), axis=-1) + rank1
    pos2 = jnp.sum(jnp.where(e2[:, None] == eids, base, 0), axis=-1) + rank2
    n_rows = 2 * t + N_EXPERTS * tmx
    jj = jnp.arange(tmx, dtype=I32)[None, :]
    dummy_pos = jnp.where(jj < (pad_cnt - counts)[:, None], (pad_start + counts)[:, None] + jj, n_rows)
    tok = jnp.arange(t, dtype=I32)
    keys = jnp.concatenate([pos1, pos2, dummy_pos.reshape(-1)])
    vals = jnp.concatenate([tok, tok, jnp.zeros((N_EXPERTS * tmx,), I32)])
    _, src = lax.sort((keys, vals), num_keys=1)
    n_tiles = n_rows // tmx
    tile_start = jnp.arange(n_tiles, dtype=I32) * tmx
    tile_expert = jnp.minimum(jnp.sum((tile_start[:, None] >= pad_end[None, :]).astype(I32), axis=-1), N_EXPERTS - 1)
    tile_valid = (tile_start < pad_end[-1]).astype(I32)
    return pos1, pos2, src, tile_expert, tile_valid


def _row_gather(idx_ref, src_hbm, dst, sem, n):
    for r in range(n):
        pltpu.make_async_copy(src_hbm.at[pl.ds(idx_ref[0, 0, r], 1), :], dst.at[pl.ds(r, 1), :], sem).start()


def _rows_wait(src_hbm, dst, sem, n):
    pltpu.make_async_copy(src_hbm.at[pl.ds(0, n), :], dst, sem).wait()


def _experts_kernel(te_ref, tv_ref, cur_ref, nxt_ref, x_hbm, w1_ref, w3_ref, w2_ref, o_ref, xbuf, sems, *, tmx):
    j = pl.program_id(0)
    n = pl.num_programs(0)
    slot = j % 2
    valid = tv_ref[j] > 0
    prev_valid = tv_ref[jnp.maximum(j - 1, 0)] > 0

    @pl.when(j == 0)
    def _():
        _row_gather(cur_ref, x_hbm, xbuf.at[0], sems.at[0], tmx)

    @pl.when(valid)
    def _():
        _row_gather(nxt_ref, x_hbm, xbuf.at[1 - slot], sems.at[1 - slot], tmx)
        _rows_wait(x_hbm, xbuf.at[slot], sems.at[slot], tmx)
        x = xbuf[slot].astype(BF16)
        o_ref[...] = _dot(_swiglu(x, w1_ref[0], w3_ref[0]), w2_ref[0])

    @pl.when(jnp.logical_not(valid))
    def _():
        o_ref[...] = jnp.zeros(o_ref.shape, F32)

    @pl.when(jnp.logical_and(jnp.logical_not(valid), jnp.logical_and(prev_valid, j > 0)))
    def _():
        _rows_wait(x_hbm, xbuf.at[slot], sems.at[slot], tmx)

    @pl.when(jnp.logical_and(valid, j == n - 1))
    def _():
        _rows_wait(x_hbm, xbuf.at[1 - slot], sems.at[1 - slot], tmx)


def _experts(hf, src, tile_expert, tile_valid, w1, w3, w2, tmx):
    t, d = hf.shape
    ne, _, ff = w1.shape
    n_tiles = src.shape[0] // tmx
    src3 = src.reshape(n_tiles, 1, tmx)
    grid_spec = pltpu.PrefetchScalarGridSpec(
        num_scalar_prefetch=2,
        grid=(n_tiles,),
        in_specs=[
            pl.BlockSpec((1, 1, tmx), lambda j, te, tv: (j, 0, 0), memory_space=pltpu.SMEM),
            pl.BlockSpec((1, 1, tmx), lambda j, te, tv: (jnp.minimum(j + 1, n_tiles - 1), 0, 0),
                         memory_space=pltpu.SMEM),
            pl.BlockSpec(memory_space=pl.ANY),
            pl.BlockSpec((1, d, ff), lambda j, te, tv: (te[j], 0, 0)),
            pl.BlockSpec((1, d, ff), lambda j, te, tv: (te[j], 0, 0)),
            pl.BlockSpec((1, ff, d), lambda j, te, tv: (te[j], 0, 0)),
        ],
        out_specs=pl.BlockSpec((tmx, d), lambda j, te, tv: (j, 0)),
        scratch_shapes=[pltpu.VMEM((2, tmx, d), F32), pltpu.SemaphoreType.DMA((2,))],
    )
    return pl.pallas_call(
        functools.partial(_experts_kernel, tmx=tmx),
        grid_spec=grid_spec,
        out_shape=jax.ShapeDtypeStruct((n_tiles * tmx, d), F32),
        compiler_params=_params(("arbitrary",)),
        name="moe_experts",
    )(tile_expert, tile_valid, src3, src3, hf, w1, w3, w2)


def _combine_kernel(p1c_ref, p2c_ref, p1n_ref, p2n_ref, hf_ref, rec_ref, o_hbm, g_ref, b_ref, of_ref, ob_ref,
                    obuf, sems, *, alpha, tmc):
    j = pl.program_id(0)
    n = pl.num_programs(0)
    slot = j % 2

    def gather(p1_ref, p2_ref, s):
        _row_gather(p1_ref, o_hbm, obuf.at[s, 0], sems.at[s], tmc)
        _row_gather(p2_ref, o_hbm, obuf.at[s, 1], sems.at[s], tmc)

    def wait(s):
        _rows_wait(o_hbm, obuf.at[s, 0], sems.at[s], tmc)
        _rows_wait(o_hbm, obuf.at[s, 1], sems.at[s], tmc)

    @pl.when(j == 0)
    def _():
        gather(p1c_ref, p2c_ref, 0)

    gather(p1n_ref, p2n_ref, 1 - slot)
    wait(slot)
    rec = rec_ref[...]
    g1 = rec[:, R_G1:R_G1 + 1]
    g2 = rec[:, R_G2:R_G2 + 1]
    y = _layer_norm(alpha * hf_ref[...] + g1 * obuf[slot, 0] + g2 * obuf[slot, 1], g_ref[...], b_ref[...])
    of_ref[...] = y
    ob_ref[...] = y.astype(BF16)

    @pl.when(j == n - 1)
    def _():
        wait(1 - slot)


def _combine(hf, rec, o_sorted, pos1, pos2, ln_g, ln_b, alpha):
    t, d = hf.shape
    tmc = _tile(t, COMBINE_TILE)
    n_tiles = t // tmc
    p1 = pos1.reshape(n_tiles, 1, tmc)
    p2 = pos2.reshape(n_tiles, 1, tmc)
    cur = pl.BlockSpec((1, 1, tmc), lambda j: (j, 0, 0), memory_space=pltpu.SMEM)
    nxt = pl.BlockSpec((1, 1, tmc), lambda j: (jnp.minimum(j + 1, n_tiles - 1), 0, 0), memory_space=pltpu.SMEM)
    row = lambda w: pl.BlockSpec((tmc, w), lambda j: (j, 0))
    return pl.pallas_call(
        functools.partial(_combine_kernel, alpha=alpha, tmc=tmc),
        grid=(n_tiles,),
        in_specs=[cur, cur, nxt, nxt, row(d), row(LANES), pl.BlockSpec(memory_space=pl.ANY),
                  pl.BlockSpec((1, d), lambda j: (0, 0)), pl.BlockSpec((1, d), lambda j: (0, 0))],
        out_specs=[row(d), row(d)],
        out_shape=[jax.ShapeDtypeStruct((t, d), F32), jax.ShapeDtypeStruct((t, d), BF16)],
        scratch_shapes=[pltpu.VMEM((2, 2, tmc, d), F32), pltpu.SemaphoreType.DMA((2,))],
        compiler_params=_params(("arbitrary",)),
        name="moe_combine",
    )(p1, p2, p1, p2, hf, rec, o_sorted, ln_g.reshape(1, d), ln_b.reshape(1, d))


def _pad_to(a, axis, mult):
    pad = (-a.shape[axis]) % mult
    if pad == 0:
        return a
    widths = [(0, 0)] * a.ndim
    widths[axis] = (0, pad)
    return jnp.pad(a, widths)


def kernel(x_prompt, x_sample, ln_emb_g, ln_emb_b, w_in, sink, lambda_q1, lambda_k1, lambda_q2, lambda_k2,
           subln_g, w_pa, w_pd, w_o, ln1_g, ln1_b, ln2_g, ln2_b, ffn_w1, ffn_w3, ffn_w2, router_w,
           moe_w1, moe_w3, moe_w2):
    bp, seq, d = x_prompt.shape
    bs, seq_s, _ = x_sample.shape
    assert seq == seq_s
    depth = w_in.shape[0]
    n_seq = bp + bs
    t = n_seq * seq
    alpha = (2.0 * depth) ** 0.25
    slopes = [2.0 ** (-8.0 * (h + 1) / N_ALIBI_HEADS) for h in range(N_ALIBI_HEADS)]
    win_slopes2 = jnp.asarray([s * LOG2E for s in slopes[:WIN_Q_HEADS]], F32)
    diff_slopes2 = jnp.asarray([s * LOG2E for s in slopes[WIN_Q_HEADS:]], F32)
    col = jnp.arange(QKV_COLS)
    is_q = (col < WIN_Q) | ((col >= DIFF_Q_COL0) & (col < DIFF_Q_COL0 + DIFF_QK))
    col_scale = jnp.where(is_q, Q_SCALE, 1.0).astype(F32).reshape(1, QKV_COLS)
    tm_route = _tile(t, ROW_TILE)

    xf, xb = _embed_ln(x_prompt.reshape(bp * seq, d), x_sample.reshape(bs * seq, d), ln_emb_g, ln_emb_b)

    for l in range(depth):
        lambda_init = 0.8 - 0.6 * math.exp(-0.3 * l)
        w_qkv = w_in[l, :, :QKV_COLS].astype(BF16)
        w_g = w_in[l, :, QKV_COLS:].astype(BF16)
        z = _inproj(xb, w_qkv, col_scale)
        ya = _win_attn(z, jnp.concatenate([sink[l].astype(F32) * LOG2E, win_slopes2]), n_seq, seq)
        yd = _diff_attn(z, diff_slopes2, lambda_q1, lambda_k1, lambda_q2, lambda_k2, subln_g,
                        n_seq, seq, l, lambda_init)
        merge_w = (w_g, w_pa[l].astype(BF16), w_pd[l].astype(BF16), w_o[l].astype(BF16), ln1_g[l], ln1_b[l], alpha)
        i = l // 2
        if l % 2 == 0:
            hf, hb = _merge(xb, ya, yd, xf, *merge_w)
            w1 = _pad_to(ffn_w1[i], 1, 512).astype(BF16)
            w3 = _pad_to(ffn_w3[i], 1, 512).astype(BF16)
            w2 = _pad_to(ffn_w2[i], 0, 512).astype(BF16)
            xf, xb = _ffn(hb, hf, w1, w3, w2, ln2_g[l], ln2_b[l], alpha)
        else:
            rw = _pad_to(router_w[i].astype(F32), 1, LANES)
            r_hi = rw.astype(BF16)
            r_lo = (rw - r_hi.astype(F32)).astype(BF16)
            hf, hb, rec, cnt = _merge(xb, ya, yd, xf, *merge_w, router=(r_hi, r_lo))
            pos1, pos2, src, tile_expert, tile_valid = _moe_plan(rec, cnt, tm_route, ROW_TILE)
            o_sorted = _experts(hf, src, tile_expert, tile_valid, moe_w1[i].astype(BF16),
                                moe_w3[i].astype(BF16), moe_w2[i].astype(BF16), ROW_TILE)
            xf, xb = _combine(hf, rec, o_sorted, pos1, pos2, ln2_g[l], ln2_b[l], alpha)

    y = xf.reshape(n_seq, seq, d)
    return (y[:bp], y[bp:])
```

```python
import functools
import math

import jax
import jax.numpy as jnp
from jax import lax
from jax.experimental import pallas as pl
from jax.experimental.pallas import tpu as pltpu

HEAD_DIM = 128
WIN_Q_HEADS = 8
WIN_KV_HEADS = 2
WIN_REP = WIN_Q_HEADS // WIN_KV_HEADS
WINDOW = 128
BLOCK = 128
DIFF_HEADS = 4
N_ALIBI_HEADS = WIN_Q_HEADS + DIFF_HEADS
WIN_Q = WIN_Q_HEADS * HEAD_DIM
WIN_KV = WIN_KV_HEADS * HEAD_DIM
DIFF_QK = DIFF_HEADS * 2 * HEAD_DIM
DIFF_V = DIFF_HEADS * 2 * HEAD_DIM
QKV_COLS = WIN_Q + 2 * WIN_KV + 2 * DIFF_QK + DIFF_V
DIFF_Q_COL0 = WIN_Q + 2 * WIN_KV
N_EXPERTS = 8
LN_EPS = 1e-5
NEG_INF = -1e30
LOG2E = 1.4426950408889634
Q_SCALE = (HEAD_DIM ** -0.5) * LOG2E
LANES = 128
R_E1, R_E2, R_G1, R_G2, R_RANK1, R_RANK2 = range(6)

VMEM_LIMIT = 56 * 1024 * 1024
F32 = jnp.float32
BF16 = jnp.bfloat16
I32 = jnp.int32

ROW_TILE = 512
FFN_ROW_TILE = 512
COMBINE_TILE = 256


def _params(sem, vmem=VMEM_LIMIT):
    return pltpu.CompilerParams(dimension_semantics=sem, vmem_limit_bytes=vmem)


def _tile(n, want):
    t = min(n, want)
    assert n % t == 0, (n, t)
    return t


def _layer_norm(v, g, b):
    mu = jnp.mean(v, axis=-1, keepdims=True)
    xc = v - mu
    var = jnp.mean(xc * xc, axis=-1, keepdims=True)
    return xc * lax.rsqrt(var + LN_EPS) * g + b


def _dot(a, b):
    return jnp.dot(a, b, preferred_element_type=F32)


def _dot_t(a, b):
    return lax.dot_general(a, b, (((1,), (1,)), ((), ())), preferred_element_type=F32)


def _embed_ln_kernel(xp_ref, xs_ref, g_ref, b_ref, of_ref, ob_ref, *, n_prompt_tiles):
    i = pl.program_id(0)

    def emit(x_ref):
        y = _layer_norm(x_ref[...], g_ref[...], b_ref[...])
        of_ref[...] = y
        ob_ref[...] = y.astype(BF16)

    @pl.when(i < n_prompt_tiles)
    def _():
        emit(xp_ref)

    @pl.when(i >= n_prompt_tiles)
    def _():
        emit(xs_ref)


def _embed_ln(xp, xs, g, b):
    tp, d = xp.shape
    ts = xs.shape[0]
    tm = _tile(math.gcd(tp, ts), 512)
    npt, nst = tp // tm, ts // tm
    t = tp + ts
    return pl.pallas_call(
        functools.partial(_embed_ln_kernel, n_prompt_tiles=npt),
        grid=(npt + nst,),
        in_specs=[
            pl.BlockSpec((tm, d), lambda i: (jnp.minimum(i, npt - 1), 0)),
            pl.BlockSpec((tm, d), lambda i: (jnp.maximum(i - npt, 0), 0)),
            pl.BlockSpec((1, d), lambda i: (0, 0)),
            pl.BlockSpec((1, d), lambda i: (0, 0)),
        ],
        out_specs=[
            pl.BlockSpec((tm, d), lambda i: (i, 0)),
            pl.BlockSpec((tm, d), lambda i: (i, 0)),
        ],
        out_shape=[jax.ShapeDtypeStruct((t, d), F32), jax.ShapeDtypeStruct((t, d), BF16)],
        compiler_params=_params(("arbitrary",)),
        name="embed_ln",
    )(xp, xs, g.reshape(1, d), b.reshape(1, d))


def _inproj_kernel(x_ref, w_ref, s_ref, o_ref):
    o_ref[...] = (_dot(x_ref[...], w_ref[...]) * s_ref[...]).astype(o_ref.dtype)


def _inproj(xb, w_in, layer, col_scale):
    t, d = xb.shape
    n = QKV_COLS
    tm = _tile(t, 1024)
    tn = _tile(n, 1536)
    return pl.pallas_call(
        _inproj_kernel,
        grid=(t // tm, n // tn),
        in_specs=[
            pl.BlockSpec((tm, d), lambda i, j: (i, 0)),
            pl.BlockSpec((None, d, tn), lambda i, j: (layer, 0, j)),
            pl.BlockSpec((1, tn), lambda i, j: (0, j)),
        ],
        out_specs=pl.BlockSpec((tm, tn), lambda i, j: (i, j)),
        out_shape=jax.ShapeDtypeStruct((t, n), BF16),
        compiler_params=_params(("parallel", "arbitrary")),
        name="inproj",
    )(xb, w_in, col_scale)


def _lane_tiles(x):
    return [x[:, u * LANES:(u + 1) * LANES] for u in range(x.shape[1] // LANES)]


def _win_attn_kernel(sc_ref, q_ref, k_ref, v_ref, o_ref, t_ref, *, seq):
    g = pl.program_id(1)
    kw = min(3 * BLOCK, seq)
    nb = seq // BLOCK
    row = lax.broadcasted_iota(I32, (BLOCK, kw), 0)
    col = lax.broadcasted_iota(I32, (BLOCK, kw), 1)
    k_starts = [min(max(n * BLOCK - BLOCK, 0), seq - kw) for n in range(nb)]
    offsets = sorted({n * BLOCK - k_starts[n] for n in range(nb)})
    dists = {off: jnp.abs(row - col + off) for off in offsets}
    head_cols = [slice(r * HEAD_DIM, (r + 1) * HEAD_DIM) for r in range(WIN_REP)]
    ms = {}
    for r, cs in enumerate(head_cols):
        sink2 = sc_ref[g * WIN_REP + r]
        slope2 = sc_ref[WIN_Q_HEADS + g * WIN_REP + r]
        bias = {off: jnp.where(d <= WINDOW, slope2 * d.astype(F32), -NEG_INF) for off, d in dists.items()}
        for n in range(nb):
            qs = slice(n * BLOCK, (n + 1) * BLOCK)
            ks = slice(k_starts[n], k_starts[n] + kw)
            t = _dot_t(q_ref[qs, cs], k_ref[ks, :]) - bias[n * BLOCK - k_starts[n]]
            t_ref[r, qs, :] = t
            mx = functools.reduce(jnp.maximum, _lane_tiles(t))
            ms[r, n] = jnp.maximum(jnp.max(mx, axis=-1, keepdims=True), sink2)
    for r, cs in enumerate(head_cols):
        sink2 = sc_ref[g * WIN_REP + r]
        for n in range(nb):
            qs = slice(n * BLOCK, (n + 1) * BLOCK)
            ks = slice(k_starts[n], k_starts[n] + kw)
            m = ms[r, n]
            p = jnp.exp2(t_ref[r, qs, :] - m)
            ls = functools.reduce(jnp.add, _lane_tiles(p))
            denom = jnp.sum(ls, axis=-1, keepdims=True) + jnp.exp2(sink2 - m)
            o_ref[qs, cs] = (_dot(p.astype(BF16), v_ref[ks, :]) / denom).astype(o_ref.dtype)


def _win_attn(z, scalars, n_seq, seq):
    gq = WIN_REP * HEAD_DIM
    kw = min(3 * BLOCK, seq)
    k_blk0 = WIN_Q // HEAD_DIM
    v_blk0 = (WIN_Q + WIN_KV) // HEAD_DIM
    return pl.pallas_call(
        functools.partial(_win_attn_kernel, seq=seq),
        grid=(n_seq, WIN_KV_HEADS),
        in_specs=[
            pl.BlockSpec(memory_space=pltpu.SMEM),
            pl.BlockSpec((seq, gq), lambda b, g: (b, g)),
            pl.BlockSpec((seq, HEAD_DIM), lambda b, g: (b, k_blk0 + g)),
            pl.BlockSpec((seq, HEAD_DIM), lambda b, g: (b, v_blk0 + g)),
        ],
        out_specs=pl.BlockSpec((seq, gq), lambda b, g: (b, g)),
        out_shape=jax.ShapeDtypeStruct((n_seq * seq, WIN_Q), BF16),
        scratch_shapes=[pltpu.VMEM((WIN_REP, seq, kw), F32)],
        compiler_params=_params(("parallel", "arbitrary")),
        name="win_attn",
    )(scalars, z, z, z)


def _diff_attn_kernel(sl_ref, lq1_ref, lk1_ref, lq2_ref, lk2_ref, g_ref, q_ref, k_ref, v_ref, o_ref,
                      bias_ref, t_ref, *, seq, tq, rb_rows, kc, layer, lambda_init):
    h = pl.program_id(0)
    qi = pl.program_id(1)
    b = pl.program_id(2)
    n_kc = seq // kc

    @pl.when(b == 0)
    def _():
        tpos = qi * tq + lax.broadcasted_iota(I32, (tq, seq), 0)
        spos = lax.broadcasted_iota(I32, (tq, seq), 1)
        bias_ref[...] = sl_ref[h] * jnp.abs(tpos - spos).astype(F32)

    lam = (jnp.exp(jnp.sum(lq1_ref[layer:layer + 1, :] * lk1_ref[layer:layer + 1, :], axis=-1, keepdims=True))
           - jnp.exp(jnp.sum(lq2_ref[layer:layer + 1, :] * lk2_ref[layer:layer + 1, :], axis=-1, keepdims=True))
           + lambda_init)

    sections = [(rb, c) for rb in range(tq // rb_rows) for c in range(2)]
    key_tiles = [slice(j * kc, (j + 1) * kc) for j in range(n_kc)]
    ms = {}
    for rb, c in sections:
        rs = slice(rb * rb_rows, (rb + 1) * rb_rows)
        cs = slice(c * HEAD_DIM, (c + 1) * HEAD_DIM)
        q = q_ref[rs, cs]
        mx = jnp.full((rb_rows, LANES), -jnp.inf, F32)
        for ks in key_tiles:
            t = _dot_t(q, k_ref[ks, cs]) - bias_ref[rs, ks]
            t_ref[c, rs, ks] = t
            mx = functools.reduce(jnp.maximum, _lane_tiles(t), mx)
        ms[rb, c] = jnp.max(mx, axis=-1, keepdims=True)
    outs = {}
    for rb, c in sections:
        rs = slice(rb * rb_rows, (rb + 1) * rb_rows)
        ls = jnp.zeros((rb_rows, LANES), F32)
        acc = jnp.zeros((rb_rows, 2 * HEAD_DIM), F32)
        for ks in key_tiles:
            p = jnp.exp2(t_ref[c, rs, ks] - ms[rb, c])
            ls = functools.reduce(jnp.add, _lane_tiles(p), ls)
            acc = acc + _dot(p.astype(BF16), v_ref[ks, :])
        outs[rb, c] = acc / jnp.sum(ls, axis=-1, keepdims=True)
    for rb in range(tq // rb_rows):
        rs = slice(rb * rb_rows, (rb + 1) * rb_rows)
        o = outs[rb, 0] - lam * outs[rb, 1]
        rms = lax.rsqrt(jnp.mean(o * o, axis=-1, keepdims=True) + LN_EPS)
        o_ref[rs, :] = (o * rms * g_ref[layer:layer + 1, :] * (1.0 - lambda_init)).astype(o_ref.dtype)


def _diff_attn(z, slopes2, lq1, lk1, lq2, lk2, subln_g, n_seq, seq, layer, lambda_init):
    hw = 2 * HEAD_DIM
    tq = _tile(seq, 512)
    rb_rows = _tile(tq, 128)
    kc = _tile(seq, 256)
    nq = seq // tq
    q_blk0 = DIFF_Q_COL0 // hw
    k_blk0 = q_blk0 + DIFF_QK // hw
    v_blk0 = k_blk0 + DIFF_QK // hw
    whole = lambda a: pl.BlockSpec(a.shape, lambda h, qi, b: (0, 0))
    return pl.pallas_call(
        functools.partial(_diff_attn_kernel, seq=seq, tq=tq, rb_rows=rb_rows, kc=kc, layer=layer,
                          lambda_init=lambda_init),
        grid=(DIFF_HEADS, nq, n_seq),
        in_specs=[
            pl.BlockSpec(memory_space=pltpu.SMEM),
            whole(lq1), whole(lk1), whole(lq2), whole(lk2), whole(subln_g),
            pl.BlockSpec((tq, hw), lambda h, qi, b: (b * nq + qi, q_blk0 + h)),
            pl.BlockSpec((seq, hw), lambda h, qi, b: (b, k_blk0 + h)),
            pl.BlockSpec((seq, hw), lambda h, qi, b: (b, v_blk0 + h)),
        ],
        out_specs=pl.BlockSpec((tq, hw), lambda h, qi, b: (b * nq + qi, h)),
        out_shape=jax.ShapeDtypeStruct((n_seq * seq, DIFF_V), BF16),
        scratch_shapes=[pltpu.VMEM((tq, seq), F32), pltpu.VMEM((2, tq, seq), F32)],
        compiler_params=_params(("arbitrary", "arbitrary", "arbitrary")),
        name="diff_attn",
    )(slopes2, lq1, lk1, lq2, lk2, subln_g, z, z, z)


def _residual_copy(x_hbm, buf, sem, tm):
    row0 = pl.multiple_of(pl.program_id(0) * tm, tm)
    return pltpu.make_async_copy(x_hbm.at[pl.ds(row0, tm), :], buf, sem)


def _route(hf, rhi_ref, rlo_ref):
    tm = hf.shape[0]
    h_hi = hf.astype(BF16)
    h_lo = (hf - h_hi.astype(F32)).astype(BF16)
    logits = _dot(h_hi, rhi_ref[...]) + _dot(h_lo, rhi_ref[...]) + _dot(h_hi, rlo_ref[...])
    lane = lax.broadcasted_iota(I32, logits.shape, 1)
    lg = jnp.where(lane < N_EXPERTS, logits, -jnp.inf)
    m1 = jnp.max(lg, axis=-1, keepdims=True)
    i1 = jnp.min(jnp.where(lg == m1, lane, LANES), axis=-1, keepdims=True)
    lg2 = jnp.where(lane == i1, -jnp.inf, lg)
    m2 = jnp.max(lg2, axis=-1, keepdims=True)
    i2 = jnp.min(jnp.where(lg2 == m2, lane, LANES), axis=-1, keepdims=True)
    e2 = jnp.exp(m2 - m1)
    g1 = 1.0 / (1.0 + e2)
    g2 = e2 / (1.0 + e2)
    oh1 = lane == i1
    oh2 = lane == i2
    sel = jnp.logical_or(oh1, oh2).astype(F32)
    r_idx = lax.broadcasted_iota(I32, (tm, tm), 0)
    c_idx = lax.broadcasted_iota(I32, (tm, tm), 1)
    lower = (c_idx < r_idx).astype(F32).astype(BF16)
    before = _dot(lower, sel.astype(BF16))
    rank1 = jnp.sum(jnp.where(oh1, before, 0.0), axis=-1, keepdims=True)
    rank2 = jnp.sum(jnp.where(oh2, before, 0.0), axis=-1, keepdims=True)
    counts = jnp.sum(sel, axis=0, keepdims=True)
    rec = jnp.zeros(logits.shape, F32)
    for ln, val in ((R_E1, i1.astype(F32)), (R_E2, i2.astype(F32)), (R_G1, g1), (R_G2, g2),
                    (R_RANK1, rank1), (R_RANK2, rank2)):
        rec = jnp.where(lane == ln, val, rec)
    return rec, counts


def _merge_kernel(*refs, alpha, with_router, tm):
    if with_router:
        (xb_ref, ya_ref, yd_ref, xf_hbm, wga_ref, wgd_ref, wpa_ref, wpd_ref, wo_ref, g_ref, b_ref,
         rhi_ref, rlo_ref, hf_ref, rec_ref, cnt_ref, xf_buf, xf_sem) = refs
    else:
        (xb_ref, ya_ref, yd_ref, xf_hbm, wga_ref, wgd_ref, wpa_ref, wpd_ref, wo_ref, g_ref, b_ref,
         hf_ref, hb_ref, xf_buf, xf_sem) = refs
    j = pl.program_id(1)

    @pl.when(j == 0)
    def _():
        _residual_copy(xf_hbm, xf_buf, xf_sem, tm).start()
        hf_ref[...] = jnp.zeros(hf_ref.shape, F32)

    x = xb_ref[...]
    ga = _dot(x, wga_ref[...])
    gd = _dot(x, wgd_ref[...])
    pa = _dot(ya_ref[...], wpa_ref[...])
    pd = _dot(yd_ref[...], wpd_ref[...])
    merged = (jax.nn.sigmoid(ga) * pa + jax.nn.sigmoid(gd) * pd).astype(BF16)
    hf_ref[...] += _dot(merged, wo_ref[...])

    @pl.when(j == pl.num_programs(1) - 1)
    def _():
        _residual_copy(xf_hbm, xf_buf, xf_sem, tm).wait()
        hf = _layer_norm(alpha * xf_buf[...] + hf_ref[...], g_ref[...], b_ref[...])
        hf_ref[...] = hf
        if with_router:
            rec, counts = _route(hf, rhi_ref, rlo_ref)
            rec_ref[...] = rec
            cnt_ref[...] = jnp.broadcast_to(counts, cnt_ref.shape)
        else:
            hb_ref[...] = hf.astype(BF16)


def _merge(xb, ya, yd, xf, w_in, w_pa, w_pd, w_o, layer, ln_g, ln_b, alpha, router=None):
    t, d = xb.shape
    tm = _tile(t, ROW_TILE)
    tn = _tile(d, 512)
    nj = d // tn
    ga_blk0 = QKV_COLS // tn
    with_router = router is not None
    row = lambda w: pl.BlockSpec((tm, w), lambda i, j: (i, 0))
    in_specs = [
        row(d), row(WIN_Q), row(DIFF_V),
        pl.BlockSpec(memory_space=pl.ANY),
        pl.BlockSpec((None, d, tn), lambda i, j: (layer, 0, ga_blk0 + j)),
        pl.BlockSpec((None, d, tn), lambda i, j: (layer, 0, ga_blk0 + nj + j)),
        pl.BlockSpec((None, WIN_Q, tn), lambda i, j: (layer, 0, j)),
        pl.BlockSpec((None, DIFF_V, tn), lambda i, j: (layer, 0, j)),
        pl.BlockSpec((None, tn, d), lambda i, j: (layer, j, 0)),
        pl.BlockSpec((1, d), lambda i, j: (0, 0)),
        pl.BlockSpec((1, d), lambda i, j: (0, 0)),
    ]
    args = [xb, ya, yd, xf, w_in, w_in, w_pa, w_pd, w_o, ln_g.reshape(1, d), ln_b.reshape(1, d)]
    out_specs = [row(d)]
    out_shape = [jax.ShapeDtypeStruct((t, d), F32)]
    if with_router:
        in_specs += [pl.BlockSpec((d, LANES), lambda i, j: (0, 0))] * 2
        args += list(router)
        out_specs += [row(LANES), pl.BlockSpec((8, LANES), lambda i, j: (i, 0))]
        out_shape += [jax.ShapeDtypeStruct((t, LANES), F32), jax.ShapeDtypeStruct((t // tm * 8, LANES), F32)]
    else:
        out_specs.append(row(d))
        out_shape.append(jax.ShapeDtypeStruct((t, d), BF16))
    return pl.pallas_call(
        functools.partial(_merge_kernel, alpha=alpha, with_router=with_router, tm=tm),
        grid=(t // tm, nj),
        in_specs=in_specs,
        out_specs=out_specs,
        out_shape=out_shape,
        scratch_shapes=[pltpu.VMEM((tm, d), F32), pltpu.SemaphoreType.DMA(())],
        compiler_params=_params(("arbitrary", "arbitrary")),
        name="merge_router" if with_router else "merge",
    )(*args)


def _swiglu(x, w1, w3):
    u = _dot(x, w1)
    return (u * jax.nn.sigmoid(u) * _dot(x, w3)).astype(BF16)


def _ffn_kernel(hb_ref, hf_hbm, w1_ref, w3_ref, w2_ref, g_ref, b_ref, of_ref, ob_ref, hf_buf, hf_sem, *, alpha, tm):
    f = pl.program_id(1)

    @pl.when(f == 0)
    def _():
        _residual_copy(hf_hbm, hf_buf, hf_sem, tm).start()
        of_ref[...] = jnp.zeros(of_ref.shape, F32)

    of_ref[...] += _dot(_swiglu(hb_ref[...], w1_ref[...], w3_ref[...]), w2_ref[...])

    @pl.when(f == pl.num_programs(1) - 1)
    def _():
        _residual_copy(hf_hbm, hf_buf, hf_sem, tm).wait()
        y = _layer_norm(alpha * hf_buf[...] + of_ref[...], g_ref[...], b_ref[...])
        of_ref[...] = y
        ob_ref[...] = y.astype(BF16)


def _ffn(hb, hf, w1, w3, w2, idx, ln_g, ln_b, alpha):
    t, d = hb.shape
    ff = w1.shape[2]
    tm = _tile(t, FFN_ROW_TILE)
    tf = _tile(ff, 512)
    row = pl.BlockSpec((tm, d), lambda i, f: (i, 0))
    return pl.pallas_call(
        functools.partial(_ffn_kernel, alpha=alpha, tm=tm),
        grid=(t // tm, ff // tf),
        in_specs=[
            row,
            pl.BlockSpec(memory_space=pl.ANY),
            pl.BlockSpec((None, d, tf), lambda i, f: (idx, 0, f)),
            pl.BlockSpec((None, d, tf), lambda i, f: (idx, 0, f)),
            pl.BlockSpec((None, tf, d), lambda i, f: (idx, f, 0)),
            pl.BlockSpec((1, d), lambda i, f: (0, 0)),
            pl.BlockSpec((1, d), lambda i, f: (0, 0)),
        ],
        out_specs=[row, row],
        out_shape=[jax.ShapeDtypeStruct((t, d), F32), jax.ShapeDtypeStruct((t, d), BF16)],
        scratch_shapes=[pltpu.VMEM((tm, d), F32), pltpu.SemaphoreType.DMA(())],
        compiler_params=_params(("arbitrary", "arbitrary")),
        name="ffn",
    )(hb, hf, w1, w3, w2, ln_g.reshape(1, d), ln_b.reshape(1, d))


def _moe_plan(rec, cnt, tm_route, tmx):
    t = rec.shape[0]
    e1 = rec[:, R_E1].astype(I32)
    e2 = rec[:, R_E2].astype(I32)
    rank1 = rec[:, R_RANK1].astype(I32)
    rank2 = rec[:, R_RANK2].astype(I32)
    tile_cnt = cnt.reshape(t // tm_route, 8, LANES)[:, 0, :N_EXPERTS].astype(I32)
    counts = jnp.sum(tile_cnt, axis=0)
    tile_base = jnp.cumsum(tile_cnt, axis=0) - tile_cnt
    pad_cnt = (counts + tmx - 1) // tmx * tmx
    pad_end = jnp.cumsum(pad_cnt)
    pad_start = pad_end - pad_cnt
    base = jnp.repeat(tile_base + pad_start[None, :], tm_route, axis=0)
    eids = jnp.arange(N_EXPERTS, dtype=I32)[None, :]
    pos1 = jnp.sum(jnp.where(e1[:, None] == eids, base, 0), axis=-1) + rank1
    pos2 = jnp.sum(jnp.where(e2[:, None] == eids, base, 0), axis=-1) + rank2
    n_rows = 2 * t + N_EXPERTS * tmx
    jj = jnp.arange(tmx, dtype=I32)[None, :]
    dummy_pos = jnp.where(jj < (pad_cnt - counts)[:, None], (pad_start + counts)[:, None] + jj, n_rows)
    tok = jnp.arange(t, dtype=I32)
    keys = jnp.concatenate([pos1, pos2, dummy_pos.reshape(-1)])
    vals = jnp.concatenate([tok, tok, jnp.zeros((N_EXPERTS * tmx,), I32)])
    _, src = lax.sort((keys, vals), num_keys=1)
    n_tiles = n_rows // tmx
    tile_start = jnp.arange(n_tiles, dtype=I32) * tmx
    tile_expert = jnp.minimum(jnp.sum((tile_start[:, None] >= pad_end[None, :]).astype(I32), axis=-1), N_EXPERTS - 1)
    tile_valid = (tile_start < pad_end[-1]).astype(I32)
    return pos1, pos2, src, tile_expert, tile_valid


def _row_gather(idx_ref, src_hbm, dst, sem, n):
    for r in range(n):
        pltpu.make_async_copy(src_hbm.at[pl.ds(idx_ref[0, 0, r], 1), :], dst.at[pl.ds(r, 1), :], sem).start()


def _rows_wait(src_hbm, dst, sem, n):
    pltpu.make_async_copy(src_hbm.at[pl.ds(0, n), :], dst, sem).wait()


def _experts_kernel(te_ref, tv_ref, cur_ref, nxt_ref, x_hbm, w1_ref, w3_ref, w2_ref, o_ref, xbuf, xb_ref, sems, *,
                    tmx):
    j = pl.program_id(0)
    n = pl.num_programs(0)
    slot = j % 2
    valid = tv_ref[j] > 0
    prev_valid = tv_ref[jnp.maximum(j - 1, 0)] > 0

    @pl.when(j == 0)
    def _():
        _row_gather(cur_ref, x_hbm, xbuf.at[0], sems.at[0], tmx)

    @pl.when(valid)
    def _():
        _rows_wait(x_hbm, xbuf.at[slot], sems.at[slot], tmx)
        xb_ref[...] = xbuf[slot].astype(BF16)
        _row_gather(nxt_ref, x_hbm, xbuf.at[1 - slot], sems.at[1 - slot], tmx)
        o_ref[...] = _dot(_swiglu(xb_ref[...], w1_ref[...], w3_ref[...]), w2_ref[...])

    @pl.when(jnp.logical_not(valid))
    def _():
        o_ref[...] = jnp.zeros(o_ref.shape, F32)

    @pl.when(jnp.logical_and(jnp.logical_not(valid), jnp.logical_and(prev_valid, j > 0)))
    def _():
        _rows_wait(x_hbm, xbuf.at[slot], sems.at[slot], tmx)

    @pl.when(jnp.logical_and(valid, j == n - 1))
    def _():
        _rows_wait(x_hbm, xbuf.at[1 - slot], sems.at[1 - slot], tmx)


def _experts(hf, src, tile_expert, tile_valid, w1, w3, w2, idx, tmx):
    t, d = hf.shape
    ff = w1.shape[3]
    n_tiles = src.shape[0] // tmx
    src3 = src.reshape(n_tiles, 1, tmx)
    grid_spec = pltpu.PrefetchScalarGridSpec(
        num_scalar_prefetch=2,
        grid=(n_tiles,),
        in_specs=[
            pl.BlockSpec((1, 1, tmx), lambda j, te, tv: (j, 0, 0), memory_space=pltpu.SMEM),
            pl.BlockSpec((1, 1, tmx), lambda j, te, tv: (jnp.minimum(j + 1, n_tiles - 1), 0, 0),
                         memory_space=pltpu.SMEM),
            pl.BlockSpec(memory_space=pl.ANY),
            pl.BlockSpec((None, None, d, ff), lambda j, te, tv: (idx, te[j], 0, 0)),
            pl.BlockSpec((None, None, d, ff), lambda j, te, tv: (idx, te[j], 0, 0)),
            pl.BlockSpec((None, None, ff, d), lambda j, te, tv: (idx, te[j], 0, 0)),
        ],
        out_specs=pl.BlockSpec((tmx, d), lambda j, te, tv: (j, 0)),
        scratch_shapes=[pltpu.VMEM((2, tmx, d), F32), pltpu.VMEM((tmx, d), BF16), pltpu.SemaphoreType.DMA((2,))],
    )
    return pl.pallas_call(
        functools.partial(_experts_kernel, tmx=tmx),
        grid_spec=grid_spec,
        out_shape=jax.ShapeDtypeStruct((n_tiles * tmx, d), F32),
        compiler_params=_params(("arbitrary",)),
        name="moe_experts",
    )(tile_expert, tile_valid, src3, src3, hf, w1, w3, w2)


def _combine_kernel(p1c_ref, p2c_ref, p1n_ref, p2n_ref, hf_ref, rec_ref, o_hbm, g_ref, b_ref, out0_ref, out1_ref,
                    obuf, sems, *, alpha, tmc, split_tile):
    j = pl.program_id(0)
    n = pl.num_programs(0)
    slot = j % 2

    def gather(p1_ref, p2_ref, s):
        _row_gather(p1_ref, o_hbm, obuf.at[s, 0], sems.at[s], tmc)
        _row_gather(p2_ref, o_hbm, obuf.at[s, 1], sems.at[s], tmc)

    def wait(s):
        _rows_wait(o_hbm, obuf.at[s, 0], sems.at[s], tmc)
        _rows_wait(o_hbm, obuf.at[s, 1], sems.at[s], tmc)

    @pl.when(j == 0)
    def _():
        gather(p1c_ref, p2c_ref, 0)

    gather(p1n_ref, p2n_ref, 1 - slot)
    wait(slot)
    rec = rec_ref[...]
    g1 = rec[:, R_G1:R_G1 + 1]
    g2 = rec[:, R_G2:R_G2 + 1]
    y = _layer_norm(alpha * hf_ref[...] + g1 * obuf[slot, 0] + g2 * obuf[slot, 1], g_ref[...], b_ref[...])
    if split_tile is None:
        out0_ref[...] = y
        out1_ref[...] = y.astype(BF16)
    else:
        @pl.when(j < split_tile)
        def _():
            out0_ref[...] = y

        @pl.when(j >= split_tile)
        def _():
            out1_ref[...] = y

    @pl.when(j == n - 1)
    def _():
        wait(1 - slot)


def _combine(hf, rec, o_sorted, pos1, pos2, ln_g, ln_b, alpha, split_rows=None):
    t, d = hf.shape
    tmc = _tile(t if split_rows is None else math.gcd(t, split_rows), COMBINE_TILE)
    n_tiles = t // tmc
    p1 = pos1.reshape(n_tiles, 1, tmc)
    p2 = pos2.reshape(n_tiles, 1, tmc)
    cur = pl.BlockSpec((1, 1, tmc), lambda j: (j, 0, 0), memory_space=pltpu.SMEM)
    nxt = pl.BlockSpec((1, 1, tmc), lambda j: (jnp.minimum(j + 1, n_tiles - 1), 0, 0), memory_space=pltpu.SMEM)
    row = lambda w: pl.BlockSpec((tmc, w), lambda j: (j, 0))
    if split_rows is None:
        split_tile = None
        out_specs = [row(d), row(d)]
        out_shape = [jax.ShapeDtypeStruct((t, d), F32), jax.ShapeDtypeStruct((t, d), BF16)]
    else:
        split_tile = split_rows // tmc
        out_specs = [pl.BlockSpec((tmc, d), lambda j: (jnp.minimum(j, split_tile - 1), 0)),
                     pl.BlockSpec((tmc, d), lambda j: (jnp.maximum(j - split_tile, 0), 0))]
        out_shape = [jax.ShapeDtypeStruct((split_rows, d), F32), jax.ShapeDtypeStruct((t - split_rows, d), F32)]
    return pl.pallas_call(
        functools.partial(_combine_kernel, alpha=alpha, tmc=tmc, split_tile=split_tile),
        grid=(n_tiles,),
        in_specs=[cur, cur, nxt, nxt, row(d), row(LANES), pl.BlockSpec(memory_space=pl.ANY),
                  pl.BlockSpec((1, d), lambda j: (0, 0)), pl.BlockSpec((1, d), lambda j: (0, 0))],
        out_specs=out_specs,
        out_shape=out_shape,
        scratch_shapes=[pltpu.VMEM((2, 2, tmc, d), F32), pltpu.SemaphoreType.DMA((2,))],
        compiler_params=_params(("arbitrary",)),
        name="moe_combine",
    )(p1, p2, p1, p2, hf, rec, o_sorted, ln_g.reshape(1, d), ln_b.reshape(1, d))


def _pad_to(a, axis, mult):
    pad = (-a.shape[axis]) % mult
    if pad == 0:
        return a
    widths = [(0, 0)] * a.ndim
    widths[axis] = (0, pad)
    return jnp.pad(a, widths)


def kernel(x_prompt, x_sample, ln_emb_g, ln_emb_b, w_in, sink, lambda_q1, lambda_k1, lambda_q2, lambda_k2,
           subln_g, w_pa, w_pd, w_o, ln1_g, ln1_b, ln2_g, ln2_b, ffn_w1, ffn_w3, ffn_w2, router_w,
           moe_w1, moe_w3, moe_w2):
    bp, seq, d = x_prompt.shape
    bs, seq_s, _ = x_sample.shape
    assert seq == seq_s
    depth = w_in.shape[0]
    n_seq = bp + bs
    t = n_seq * seq
    alpha = (2.0 * depth) ** 0.25
    slopes = [2.0 ** (-8.0 * (h + 1) / N_ALIBI_HEADS) for h in range(N_ALIBI_HEADS)]
    win_slopes2 = jnp.asarray([s * LOG2E for s in slopes[:WIN_Q_HEADS]], F32)
    diff_slopes2 = jnp.asarray([s * LOG2E for s in slopes[WIN_Q_HEADS:]], F32)
    col = jnp.arange(QKV_COLS)
    is_q = (col < WIN_Q) | ((col >= DIFF_Q_COL0) & (col < DIFF_Q_COL0 + DIFF_QK))
    col_scale = jnp.where(is_q, Q_SCALE, 1.0).astype(F32).reshape(1, QKV_COLS)
    tm_route = _tile(t, ROW_TILE)

    w_in_b, w_pa_b, w_pd_b, w_o_b = (w.astype(BF16) for w in (w_in, w_pa, w_pd, w_o))
    ffn_w1_b = _pad_to(ffn_w1, 2, 512).astype(BF16)
    ffn_w3_b = _pad_to(ffn_w3, 2, 512).astype(BF16)
    ffn_w2_b = _pad_to(ffn_w2, 1, 512).astype(BF16)
    moe_w1_b, moe_w3_b, moe_w2_b = (w.astype(BF16) for w in (moe_w1, moe_w3, moe_w2))

    xf, xb = _embed_ln(x_prompt.reshape(bp * seq, d), x_sample.reshape(bs * seq, d), ln_emb_g, ln_emb_b)

    outs = None
    for l in range(depth):
        lambda_init = 0.8 - 0.6 * math.exp(-0.3 * l)
        z = _inproj(xb, w_in_b, l, col_scale)
        ya = _win_attn(z, jnp.concatenate([sink[l].astype(F32) * LOG2E, win_slopes2]), n_seq, seq)
        yd = _diff_attn(z, diff_slopes2, lambda_q1, lambda_k1, lambda_q2, lambda_k2, subln_g,
                        n_seq, seq, l, lambda_init)
        merge_args = (xb, ya, yd, xf, w_in_b, w_pa_b, w_pd_b, w_o_b, l, ln1_g[l], ln1_b[l], alpha)
        i = l // 2
        last = l == depth - 1
        if l % 2 == 0:
            hf, hb = _merge(*merge_args)
            xf, xb = _ffn(hb, hf, ffn_w1_b, ffn_w3_b, ffn_w2_b, i, ln2_g[l], ln2_b[l], alpha)
        else:
            rw = _pad_to(router_w[i].astype(F32), 1, LANES)
            r_hi = rw.astype(BF16)
            r_lo = (rw - r_hi.astype(F32)).astype(BF16)
            hf, rec, cnt = _merge(*merge_args, router=(r_hi, r_lo))
            pos1, pos2, src, tile_expert, tile_valid = _moe_plan(rec, cnt, tm_route, ROW_TILE)
            o_sorted = _experts(hf, src, tile_expert, tile_valid, moe_w1_b, moe_w3_b, moe_w2_b, i, ROW_TILE)
            res = _combine(hf, rec, o_sorted, pos1, pos2, ln2_g[l], ln2_b[l], alpha,
                           split_rows=bp * seq if last else None)
            if last:
                outs = res
            else:
                xf, xb = res

    if outs is None:
        outs = (xf[:bp * seq], xf[bp * seq:])
    return (outs[0].reshape(bp, seq, d), outs[1].reshape(bs, seq, d))
```

```python
import functools
import math

import jax
import jax.numpy as jnp
from jax import lax
from jax.experimental import pallas as pl
from jax.experimental.pallas import tpu as pltpu

HEAD_DIM = 128
WIN_Q_HEADS = 8
WIN_KV_HEADS = 2
WIN_REP = WIN_Q_HEADS // WIN_KV_HEADS
WINDOW = 128
BLOCK = 128
DIFF_HEADS = 4
N_ALIBI_HEADS = WIN_Q_HEADS + DIFF_HEADS
WIN_Q = WIN_Q_HEADS * HEAD_DIM
WIN_KV = WIN_KV_HEADS * HEAD_DIM
DIFF_QK = DIFF_HEADS * 2 * HEAD_DIM
DIFF_V = DIFF_HEADS * 2 * HEAD_DIM
QKV_COLS = WIN_Q + 2 * WIN_KV + 2 * DIFF_QK + DIFF_V
DIFF_Q_COL0 = WIN_Q + 2 * WIN_KV
N_EXPERTS = 8
LN_EPS = 1e-5
NEG_INF = -1e30
LOG2E = 1.4426950408889634
Q_SCALE = (HEAD_DIM ** -0.5) * LOG2E
LANES = 128
R_E1, R_E2, R_G1, R_G2, R_RANK1, R_RANK2 = range(6)

VMEM_LIMIT = 56 * 1024 * 1024
F32 = jnp.float32
BF16 = jnp.bfloat16
I32 = jnp.int32

ROW_TILE = 512
FFN_ROW_TILE = 512
COMBINE_TILE = 256


def _params(sem, vmem=VMEM_LIMIT):
    return pltpu.CompilerParams(dimension_semantics=sem, vmem_limit_bytes=vmem)


def _tile(n, want):
    t = min(n, want)
    assert n % t == 0, (n, t)
    return t


def _layer_norm(v, g, b):
    mu = jnp.mean(v, axis=-1, keepdims=True)
    xc = v - mu
    var = jnp.mean(xc * xc, axis=-1, keepdims=True)
    return xc * lax.rsqrt(var + LN_EPS) * g + b


def _dot(a, b):
    return jnp.dot(a, b, preferred_element_type=F32)


def _dot_t(a, b):
    return lax.dot_general(a, b, (((1,), (1,)), ((), ())), preferred_element_type=F32)


def _embed_ln_kernel(xp_ref, xs_ref, g_ref, b_ref, of_ref, ob_ref, *, n_prompt_tiles):
    i = pl.program_id(0)

    def emit(x_ref):
        y = _layer_norm(x_ref[...], g_ref[...], b_ref[...])
        of_ref[...] = y
        ob_ref[...] = y.astype(BF16)

    @pl.when(i < n_prompt_tiles)
    def _():
        emit(xp_ref)

    @pl.when(i >= n_prompt_tiles)
    def _():
        emit(xs_ref)


def _embed_ln(xp, xs, g, b):
    tp, d = xp.shape
    ts = xs.shape[0]
    tm = _tile(math.gcd(tp, ts), 512)
    npt, nst = tp // tm, ts // tm
    t = tp + ts
    return pl.pallas_call(
        functools.partial(_embed_ln_kernel, n_prompt_tiles=npt),
        grid=(npt + nst,),
        in_specs=[
            pl.BlockSpec((tm, d), lambda i: (jnp.minimum(i, npt - 1), 0)),
            pl.BlockSpec((tm, d), lambda i: (jnp.maximum(i - npt, 0), 0)),
            pl.BlockSpec((1, d), lambda i: (0, 0)),
            pl.BlockSpec((1, d), lambda i: (0, 0)),
        ],
        out_specs=[
            pl.BlockSpec((tm, d), lambda i: (i, 0)),
            pl.BlockSpec((tm, d), lambda i: (i, 0)),
        ],
        out_shape=[jax.ShapeDtypeStruct((t, d), F32), jax.ShapeDtypeStruct((t, d), BF16)],
        compiler_params=_params(("arbitrary",)),
        name="embed_ln",
    )(xp, xs, g.reshape(1, d), b.reshape(1, d))


def _inproj_kernel(x_ref, w_ref, s_ref, o_ref):
    o_ref[...] = (_dot(x_ref[...], w_ref[...]) * s_ref[...]).astype(o_ref.dtype)


def _inproj(xb, w_in, layer, col_scale):
    t, d = xb.shape
    n = QKV_COLS
    tm = _tile(t, 1024)
    tn = _tile(n, 1536)
    return pl.pallas_call(
        _inproj_kernel,
        grid=(t // tm, n // tn),
        in_specs=[
            pl.BlockSpec((tm, d), lambda i, j: (i, 0)),
            pl.BlockSpec((None, d, tn), lambda i, j: (layer, 0, j)),
            pl.BlockSpec((1, tn), lambda i, j: (0, j)),
        ],
        out_specs=pl.BlockSpec((tm, tn), lambda i, j: (i, j)),
        out_shape=jax.ShapeDtypeStruct((t, n), BF16),
        compiler_params=_params(("parallel", "arbitrary")),
        name="inproj",
    )(xb, w_in, col_scale)


def _lane_tiles(x):
    return [x[:, u * LANES:(u + 1) * LANES] for u in range(x.shape[1] // LANES)]


def _win_attn_kernel(sc_ref, q_ref, k_ref, v_ref, o_ref, t_ref, *, seq):
    g = pl.program_id(1)
    kw = min(3 * BLOCK, seq)
    nb = seq // BLOCK
    row = lax.broadcasted_iota(I32, (BLOCK, kw), 0)
    col = lax.broadcasted_iota(I32, (BLOCK, kw), 1)
    k_starts = [min(max(n * BLOCK - BLOCK, 0), seq - kw) for n in range(nb)]
    offsets = sorted({n * BLOCK - k_starts[n] for n in range(nb)})
    dists = {off: jnp.abs(row - col + off) for off in offsets}
    head_cols = [slice(r * HEAD_DIM, (r + 1) * HEAD_DIM) for r in range(WIN_REP)]
    ms = {}
    for r, cs in enumerate(head_cols):
        sink2 = sc_ref[g * WIN_REP + r]
        slope2 = sc_ref[WIN_Q_HEADS + g * WIN_REP + r]
        bias = {off: jnp.where(d <= WINDOW, slope2 * d.astype(F32), -NEG_INF) for off, d in dists.items()}
        for n in range(nb):
            qs = slice(n * BLOCK, (n + 1) * BLOCK)
            ks = slice(k_starts[n], k_starts[n] + kw)
            t = _dot_t(q_ref[qs, cs], k_ref[ks, :]) - bias[n * BLOCK - k_starts[n]]
            t_ref[r, qs, :] = t
            mx = functools.reduce(jnp.maximum, _lane_tiles(t))
            ms[r, n] = jnp.maximum(jnp.max(mx, axis=-1, keepdims=True), sink2)
    for r, cs in enumerate(head_cols):
        sink2 = sc_ref[g * WIN_REP + r]
        for n in range(nb):
            qs = slice(n * BLOCK, (n + 1) * BLOCK)
            ks = slice(k_starts[n], k_starts[n] + kw)
            m = ms[r, n]
            p = jnp.exp2(t_ref[r, qs, :] - m)
            ls = functools.reduce(jnp.add, _lane_tiles(p))
            denom = jnp.sum(ls, axis=-1, keepdims=True) + jnp.exp2(sink2 - m)
            o_ref[qs, cs] = (_dot(p.astype(BF16), v_ref[ks, :]) / denom).astype(o_ref.dtype)


def _win_attn(z, scalars, n_seq, seq):
    gq = WIN_REP * HEAD_DIM
    kw = min(3 * BLOCK, seq)
    k_blk0 = WIN_Q // HEAD_DIM
    v_blk0 = (WIN_Q + WIN_KV) // HEAD_DIM
    return pl.pallas_call(
        functools.partial(_win_attn_kernel, seq=seq),
        grid=(n_seq, WIN_KV_HEADS),
        in_specs=[
            pl.BlockSpec(memory_space=pltpu.SMEM),
            pl.BlockSpec((seq, gq), lambda b, g: (b, g)),
            pl.BlockSpec((seq, HEAD_DIM), lambda b, g: (b, k_blk0 + g)),
            pl.BlockSpec((seq, HEAD_DIM), lambda b, g: (b, v_blk0 + g)),
        ],
        out_specs=pl.BlockSpec((seq, gq), lambda b, g: (b, g)),
        out_shape=jax.ShapeDtypeStruct((n_seq * seq, WIN_Q), BF16),
        scratch_shapes=[pltpu.VMEM((WIN_REP, seq, kw), F32)],
        compiler_params=_params(("parallel", "arbitrary")),
        name="win_attn",
    )(scalars, z, z, z)


def _diff_attn_kernel(sl_ref, lq1_ref, lk1_ref, lq2_ref, lk2_ref, g_ref, q_ref, k_ref, v_ref, o_ref,
                      bias_ref, t_ref, *, seq, tq, rb_rows, kc, layer, lambda_init):
    h = pl.program_id(0)
    qi = pl.program_id(1)
    b = pl.program_id(2)
    n_kc = seq // kc

    @pl.when(b == 0)
    def _():
        tpos = qi * tq + lax.broadcasted_iota(I32, (tq, seq), 0)
        spos = lax.broadcasted_iota(I32, (tq, seq), 1)
        bias_ref[...] = sl_ref[h] * jnp.abs(tpos - spos).astype(F32)

    lam = (jnp.exp(jnp.sum(lq1_ref[layer:layer + 1, :] * lk1_ref[layer:layer + 1, :], axis=-1, keepdims=True))
           - jnp.exp(jnp.sum(lq2_ref[layer:layer + 1, :] * lk2_ref[layer:layer + 1, :], axis=-1, keepdims=True))
           + lambda_init)

    sections = [(rb, c) for rb in range(tq // rb_rows) for c in range(2)]
    key_tiles = [slice(j * kc, (j + 1) * kc) for j in range(n_kc)]
    ms = {}
    for rb, c in sections:
        rs = slice(rb * rb_rows, (rb + 1) * rb_rows)
        cs = slice(c * HEAD_DIM, (c + 1) * HEAD_DIM)
        q = q_ref[rs, cs]
        mx = jnp.full((rb_rows, LANES), -jnp.inf, F32)
        for ks in key_tiles:
            t = _dot_t(q, k_ref[ks, cs]) - bias_ref[rs, ks]
            t_ref[c, rs, ks] = t
            mx = functools.reduce(jnp.maximum, _lane_tiles(t), mx)
        ms[rb, c] = jnp.max(mx, axis=-1, keepdims=True)
    outs = {}
    for rb, c in sections:
        rs = slice(rb * rb_rows, (rb + 1) * rb_rows)
        ls = jnp.zeros((rb_rows, LANES), F32)
        acc = jnp.zeros((rb_rows, 2 * HEAD_DIM), F32)
        for ks in key_tiles:
            p = jnp.exp2(t_ref[c, rs, ks] - ms[rb, c])
            ls = functools.reduce(jnp.add, _lane_tiles(p), ls)
            acc = acc + _dot(p.astype(BF16), v_ref[ks, :])
        outs[rb, c] = acc / jnp.sum(ls, axis=-1, keepdims=True)
    for rb in range(tq // rb_rows):
        rs = slice(rb * rb_rows, (rb + 1) * rb_rows)
        o = outs[rb, 0] - lam * outs[rb, 1]
        rms = lax.rsqrt(jnp.mean(o * o, axis=-1, keepdims=True) + LN_EPS)
        o_ref[rs, :] = (o * rms * g_ref[layer:layer + 1, :] * (1.0 - lambda_init)).astype(o_ref.dtype)


def _diff_attn(z, slopes2, lq1, lk1, lq2, lk2, subln_g, n_seq, seq, layer, lambda_init):
    hw = 2 * HEAD_DIM
    tq = _tile(seq, 1024)
    rb_rows = _tile(tq, 128)
    kc = _tile(seq, 256)
    nq = seq // tq
    q_blk0 = DIFF_Q_COL0 // hw
    k_blk0 = q_blk0 + DIFF_QK // hw
    v_blk0 = k_blk0 + DIFF_QK // hw
    whole = lambda a: pl.BlockSpec(a.shape, lambda h, qi, b: (0, 0))
    return pl.pallas_call(
        functools.partial(_diff_attn_kernel, seq=seq, tq=tq, rb_rows=rb_rows, kc=kc, layer=layer,
                          lambda_init=lambda_init),
        grid=(DIFF_HEADS, nq, n_seq),
        in_specs=[
            pl.BlockSpec(memory_space=pltpu.SMEM),
            whole(lq1), whole(lk1), whole(lq2), whole(lk2), whole(subln_g),
            pl.BlockSpec((tq, hw), lambda h, qi, b: (b * nq + qi, q_blk0 + h)),
            pl.BlockSpec((seq, hw), lambda h, qi, b: (b, k_blk0 + h)),
            pl.BlockSpec((seq, hw), lambda h, qi, b: (b, v_blk0 + h)),
        ],
        out_specs=pl.BlockSpec((tq, hw), lambda h, qi, b: (b * nq + qi, h)),
        out_shape=jax.ShapeDtypeStruct((n_seq * seq, DIFF_V), BF16),
        scratch_shapes=[pltpu.VMEM((tq, seq), F32), pltpu.VMEM((2, tq, seq), F32)],
        compiler_params=_params(("arbitrary", "arbitrary", "arbitrary")),
        name="diff_attn",
    )(slopes2, lq1, lk1, lq2, lk2, subln_g, z, z, z)


def _residual_copy(x_hbm, buf, sem, tm):
    row0 = pl.multiple_of(pl.program_id(0) * tm, tm)
    return pltpu.make_async_copy(x_hbm.at[pl.ds(row0, tm), :], buf, sem)


def _route(hf, rhi_ref, rlo_ref):
    tm = hf.shape[0]
    h_hi = hf.astype(BF16)
    h_lo = (hf - h_hi.astype(F32)).astype(BF16)
    logits = _dot(h_hi, rhi_ref[...]) + _dot(h_lo, rhi_ref[...]) + _dot(h_hi, rlo_ref[...])
    lane = lax.broadcasted_iota(I32, logits.shape, 1)
    lg = jnp.where(lane < N_EXPERTS, logits, -jnp.inf)
    m1 = jnp.max(lg, axis=-1, keepdims=True)
    i1 = jnp.min(jnp.where(lg == m1, lane, LANES), axis=-1, keepdims=True)
    lg2 = jnp.where(lane == i1, -jnp.inf, lg)
    m2 = jnp.max(lg2, axis=-1, keepdims=True)
    i2 = jnp.min(jnp.where(lg2 == m2, lane, LANES), axis=-1, keepdims=True)
    e2 = jnp.exp(m2 - m1)
    g1 = 1.0 / (1.0 + e2)
    g2 = e2 / (1.0 + e2)
    oh1 = lane == i1
    oh2 = lane == i2
    sel = jnp.logical_or(oh1, oh2).astype(F32)
    r_idx = lax.broadcasted_iota(I32, (tm, tm), 0)
    c_idx = lax.broadcasted_iota(I32, (tm, tm), 1)
    lower = (c_idx < r_idx).astype(F32).astype(BF16)
    before = _dot(lower, sel.astype(BF16))
    rank1 = jnp.sum(jnp.where(oh1, before, 0.0), axis=-1, keepdims=True)
    rank2 = jnp.sum(jnp.where(oh2, before, 0.0), axis=-1, keepdims=True)
    counts = jnp.sum(sel, axis=0, keepdims=True)
    rec = jnp.zeros(logits.shape, F32)
    for ln, val in ((R_E1, i1.astype(F32)), (R_E2, i2.astype(F32)), (R_G1, g1), (R_G2, g2),
                    (R_RANK1, rank1), (R_RANK2, rank2)):
        rec = jnp.where(lane == ln, val, rec)
    return rec, counts


def _merge_kernel(*refs, alpha, with_router, tm):
    if with_router:
        (xb_ref, ya_ref, yd_ref, xf_hbm, wga_ref, wgd_ref, wpa_ref, wpd_ref, wo_ref, g_ref, b_ref,
         rhi_ref, rlo_ref, hf_ref, rec_ref, cnt_ref, xf_buf, xf_sem) = refs
    else:
        (xb_ref, ya_ref, yd_ref, xf_hbm, wga_ref, wgd_ref, wpa_ref, wpd_ref, wo_ref, g_ref, b_ref,
         hf_ref, hb_ref, xf_buf, xf_sem) = refs
    j = pl.program_id(1)

    @pl.when(j == 0)
    def _():
        _residual_copy(xf_hbm, xf_buf, xf_sem, tm).start()
        hf_ref[...] = jnp.zeros(hf_ref.shape, F32)

    x = xb_ref[...]
    ga = _dot(x, wga_ref[...])
    gd = _dot(x, wgd_ref[...])
    pa = _dot(ya_ref[...], wpa_ref[...])
    pd = _dot(yd_ref[...], wpd_ref[...])
    merged = (jax.nn.sigmoid(ga) * pa + jax.nn.sigmoid(gd) * pd).astype(BF16)
    hf_ref[...] += _dot(merged, wo_ref[...])

    @pl.when(j == pl.num_programs(1) - 1)
    def _():
        _residual_copy(xf_hbm, xf_buf, xf_sem, tm).wait()
        hf = _layer_norm(alpha * xf_buf[...] + hf_ref[...], g_ref[...], b_ref[...])
        hf_ref[...] = hf
        if with_router:
            rec, counts = _route(hf, rhi_ref, rlo_ref)
            rec_ref[...] = rec
            cnt_ref[...] = jnp.broadcast_to(counts, cnt_ref.shape)
        else:
            hb_ref[...] = hf.astype(BF16)


def _merge(xb, ya, yd, xf, w_in, w_pa, w_pd, w_o, layer, ln_g, ln_b, alpha, router=None):
    t, d = xb.shape
    tm = _tile(t, ROW_TILE)
    tn = _tile(d, 512)
    nj = d // tn
    ga_blk0 = QKV_COLS // tn
    with_router = router is not None
    row = lambda w: pl.BlockSpec((tm, w), lambda i, j: (i, 0))
    in_specs = [
        row(d), row(WIN_Q), row(DIFF_V),
        pl.BlockSpec(memory_space=pl.ANY),
        pl.BlockSpec((None, d, tn), lambda i, j: (layer, 0, ga_blk0 + j)),
        pl.BlockSpec((None, d, tn), lambda i, j: (layer, 0, ga_blk0 + nj + j)),
        pl.BlockSpec((None, WIN_Q, tn), lambda i, j: (layer, 0, j)),
        pl.BlockSpec((None, DIFF_V, tn), lambda i, j: (layer, 0, j)),
        pl.BlockSpec((None, tn, d), lambda i, j: (layer, j, 0)),
        pl.BlockSpec((1, d), lambda i, j: (0, 0)),
        pl.BlockSpec((1, d), lambda i, j: (0, 0)),
    ]
    args = [xb, ya, yd, xf, w_in, w_in, w_pa, w_pd, w_o, ln_g.reshape(1, d), ln_b.reshape(1, d)]
    out_specs = [row(d)]
    out_shape = [jax.ShapeDtypeStruct((t, d), F32)]
    if with_router:
        in_specs += [pl.BlockSpec((d, LANES), lambda i, j: (0, 0))] * 2
        args += list(router)
        out_specs += [row(LANES), pl.BlockSpec((8, LANES), lambda i, j: (i, 0))]
        out_shape += [jax.ShapeDtypeStruct((t, LANES), F32), jax.ShapeDtypeStruct((t // tm * 8, LANES), F32)]
    else:
        out_specs.append(row(d))
        out_shape.append(jax.ShapeDtypeStruct((t, d), BF16))
    return pl.pallas_call(
        functools.partial(_merge_kernel, alpha=alpha, with_router=with_router, tm=tm),
        grid=(t // tm, nj),
        in_specs=in_specs,
        out_specs=out_specs,
        out_shape=out_shape,
        scratch_shapes=[pltpu.VMEM((tm, d), F32), pltpu.SemaphoreType.DMA(())],
        compiler_params=_params(("arbitrary", "arbitrary")),
        name="merge_router" if with_router else "merge",
    )(*args)


def _swiglu(x, w1, w3):
    u = _dot(x, w1)
    return (u * jax.nn.sigmoid(u) * _dot(x, w3)).astype(BF16)


def _ffn_kernel(hb_ref, hf_hbm, w1_ref, w3_ref, w2_ref, g_ref, b_ref, of_ref, ob_ref, hf_buf, hf_sem, *, alpha, tm):
    f = pl.program_id(1)

    @pl.when(f == 0)
    def _():
        _residual_copy(hf_hbm, hf_buf, hf_sem, tm).start()
        of_ref[...] = jnp.zeros(of_ref.shape, F32)

    of_ref[...] += _dot(_swiglu(hb_ref[...], w1_ref[...], w3_ref[...]), w2_ref[...])

    @pl.when(f == pl.num_programs(1) - 1)
    def _():
        _residual_copy(hf_hbm, hf_buf, hf_sem, tm).wait()
        y = _layer_norm(alpha * hf_buf[...] + of_ref[...], g_ref[...], b_ref[...])
        of_ref[...] = y
        ob_ref[...] = y.astype(BF16)


def _ffn(hb, hf, w1, w3, w2, idx, ln_g, ln_b, alpha):
    t, d = hb.shape
    ff = w1.shape[2]
    tm = _tile(t, FFN_ROW_TILE)
    tf = _tile(ff, 512)
    row = pl.BlockSpec((tm, d), lambda i, f: (i, 0))
    return pl.pallas_call(
        functools.partial(_ffn_kernel, alpha=alpha, tm=tm),
        grid=(t // tm, ff // tf),
        in_specs=[
            row,
            pl.BlockSpec(memory_space=pl.ANY),
            pl.BlockSpec((None, d, tf), lambda i, f: (idx, 0, f)),
            pl.BlockSpec((None, d, tf), lambda i, f: (idx, 0, f)),
            pl.BlockSpec((None, tf, d), lambda i, f: (idx, f, 0)),
            pl.BlockSpec((1, d), lambda i, f: (0, 0)),
            pl.BlockSpec((1, d), lambda i, f: (0, 0)),
        ],
        out_specs=[row, row],
        out_shape=[jax.ShapeDtypeStruct((t, d), F32), jax.ShapeDtypeStruct((t, d), BF16)],
        scratch_shapes=[pltpu.VMEM((tm, d), F32), pltpu.SemaphoreType.DMA(())],
        compiler_params=_params(("arbitrary", "arbitrary")),
        name="ffn",
    )(hb, hf, w1, w3, w2, ln_g.reshape(1, d), ln_b.reshape(1, d))


def _moe_plan(rec, cnt, tm_route, tmx):
    t = rec.shape[0]
    e1 = rec[:, R_E1].astype(I32)
    e2 = rec[:, R_E2].astype(I32)
    rank1 = rec[:, R_RANK1].astype(I32)
    rank2 = rec[:, R_RANK2].astype(I32)
    tile_cnt = cnt.reshape(t // tm_route, 8, LANES)[:, 0, :N_EXPERTS].astype(I32)
    counts = jnp.sum(tile_cnt, axis=0)
    tile_base = jnp.cumsum(tile_cnt, axis=0) - tile_cnt
    pad_cnt = (counts + tmx - 1) // tmx * tmx
    pad_end = jnp.cumsum(pad_cnt)
    pad_start = pad_end - pad_cnt
    base = jnp.repeat(tile_base + pad_start[None, :], tm_route, axis=0)
    eids = jnp.arange(N_EXPERTS, dtype=I32)[None, :]
    pos1 = jnp.sum(jnp.where(e1[:, None] == eids, base, 0), axis=-1) + rank1
    pos2 = jnp.sum(jnp.where(e2[:, None] == eids, base, 0), axis=-1) + rank2
    n_rows = 2 * t + N_EXPERTS * tmx
    jj = jnp.arange(tmx, dtype=I32)[None, :]
    dummy_pos = jnp.where(jj < (pad_cnt - counts)[:, None], (pad_start + counts)[:, None] + jj, n_rows)
    tok = jnp.arange(t, dtype=I32)
    keys = jnp.concatenate([pos1, pos2, dummy_pos.reshape(-1)])
    vals = jnp.concatenate([tok, tok, jnp.zeros((N_EXPERTS * tmx,), I32)])
    _, src = lax.sort((keys, vals), num_keys=1)
    n_tiles = n_rows // tmx
    tile_start = jnp.arange(n_tiles, dtype=I32) * tmx
    tile_expert = jnp.minimum(jnp.sum((tile_start[:, None] >= pad_end[None, :]).astype(I32), axis=-1), N_EXPERTS - 1)
    tile_valid = (tile_start < pad_end[-1]).astype(I32)
    return pos1, pos2, src, tile_expert, tile_valid


def _row_gather(idx_ref, src_hbm, dst, sem, n):
    for r in range(n):
        pltpu.make_async_copy(src_hbm.at[pl.ds(idx_ref[0, 0, r], 1), :], dst.at[pl.ds(r, 1), :], sem).start()


def _rows_wait(src_hbm, dst, sem, n):
    pltpu.make_async_copy(src_hbm.at[pl.ds(0, n), :], dst, sem).wait()


def _experts_kernel(te_ref, tv_ref, cur_ref, nxt_ref, x_hbm, w1_ref, w3_ref, w2_ref, o_ref, xbuf, xb_ref, sems, *,
                    tmx):
    j = pl.program_id(0)
    n = pl.num_programs(0)
    slot = j % 2
    valid = tv_ref[j] > 0
    prev_valid = tv_ref[jnp.maximum(j - 1, 0)] > 0

    @pl.when(j == 0)
    def _():
        _row_gather(cur_ref, x_hbm, xbuf.at[0], sems.at[0], tmx)

    @pl.when(valid)
    def _():
        _row_gather(nxt_ref, x_hbm, xbuf.at[1 - slot], sems.at[1 - slot], tmx)
        _rows_wait(x_hbm, xbuf.at[slot], sems.at[slot], tmx)
        xb_ref[...] = xbuf[slot].astype(BF16)
        o_ref[...] = _dot(_swiglu(xb_ref[...], w1_ref[...], w3_ref[...]), w2_ref[...])

    @pl.when(jnp.logical_not(valid))
    def _():
        o_ref[...] = jnp.zeros(o_ref.shape, F32)

    @pl.when(jnp.logical_and(jnp.logical_not(valid), jnp.logical_and(prev_valid, j > 0)))
    def _():
        _rows_wait(x_hbm, xbuf.at[slot], sems.at[slot], tmx)

    @pl.when(jnp.logical_and(valid, j == n - 1))
    def _():
        _rows_wait(x_hbm, xbuf.at[1 - slot], sems.at[1 - slot], tmx)


def _experts(hf, src, tile_expert, tile_valid, w1, w3, w2, idx, tmx):
    t, d = hf.shape
    ff = w1.shape[3]
    n_tiles = src.shape[0] // tmx
    src3 = src.reshape(n_tiles, 1, tmx)
    grid_spec = pltpu.PrefetchScalarGridSpec(
        num_scalar_prefetch=2,
        grid=(n_tiles,),
        in_specs=[
            pl.BlockSpec((1, 1, tmx), lambda j, te, tv: (j, 0, 0), memory_space=pltpu.SMEM),
            pl.BlockSpec((1, 1, tmx), lambda j, te, tv: (jnp.minimum(j + 1, n_tiles - 1), 0, 0),
                         memory_space=pltpu.SMEM),
            pl.BlockSpec(memory_space=pl.ANY),
            pl.BlockSpec((None, None, d, ff), lambda j, te, tv: (idx, te[j], 0, 0)),
            pl.BlockSpec((None, None, d, ff), lambda j, te, tv: (idx, te[j], 0, 0)),
            pl.BlockSpec((None, None, ff, d), lambda j, te, tv: (idx, te[j], 0, 0)),
        ],
        out_specs=pl.BlockSpec((tmx, d), lambda j, te, tv: (j, 0)),
        scratch_shapes=[pltpu.VMEM((2, tmx, d), F32), pltpu.VMEM((tmx, d), BF16), pltpu.SemaphoreType.DMA((2,))],
    )
    return pl.pallas_call(
        functools.partial(_experts_kernel, tmx=tmx),
        grid_spec=grid_spec,
        out_shape=jax.ShapeDtypeStruct((n_tiles * tmx, d), F32),
        compiler_params=_params(("arbitrary",)),
        name="moe_experts",
    )(tile_expert, tile_valid, src3, src3, hf, w1, w3, w2)


def _combine_kernel(p1c_ref, p2c_ref, p1n_ref, p2n_ref, hf_ref, rec_ref, o_hbm, g_ref, b_ref, out0_ref, out1_ref,
                    obuf, sems, *, alpha, tmc, split_tile):
    j = pl.program_id(0)
    n = pl.num_programs(0)
    slot = j % 2

    def gather(p1_ref, p2_ref, s):
        _row_gather(p1_ref, o_hbm, obuf.at[s, 0], sems.at[s], tmc)
        _row_gather(p2_ref, o_hbm, obuf.at[s, 1], sems.at[s], tmc)

    def wait(s):
        _rows_wait(o_hbm, obuf.at[s, 0], sems.at[s], tmc)
        _rows_wait(o_hbm, obuf.at[s, 1], sems.at[s], tmc)

    @pl.when(j == 0)
    def _():
        gather(p1c_ref, p2c_ref, 0)

    gather(p1n_ref, p2n_ref, 1 - slot)
    wait(slot)
    rec = rec_ref[...]
    g1 = rec[:, R_G1:R_G1 + 1]
    g2 = rec[:, R_G2:R_G2 + 1]
    y = _layer_norm(alpha * hf_ref[...] + g1 * obuf[slot, 0] + g2 * obuf[slot, 1], g_ref[...], b_ref[...])
    if split_tile is None:
        out0_ref[...] = y
        out1_ref[...] = y.astype(BF16)
    else:
        @pl.when(j < split_tile)
        def _():
            out0_ref[...] = y

        @pl.when(j >= split_tile)
        def _():
            out1_ref[...] = y

    @pl.when(j == n - 1)
    def _():
        wait(1 - slot)


def _combine(hf, rec, o_sorted, pos1, pos2, ln_g, ln_b, alpha, split_rows=None):
    t, d = hf.shape
    tmc = _tile(t if split_rows is None else math.gcd(t, split_rows), COMBINE_TILE)
    n_tiles = t // tmc
    p1 = pos1.reshape(n_tiles, 1, tmc)
    p2 = pos2.reshape(n_tiles, 1, tmc)
    cur = pl.BlockSpec((1, 1, tmc), lambda j: (j, 0, 0), memory_space=pltpu.SMEM)
    nxt = pl.BlockSpec((1, 1, tmc), lambda j: (jnp.minimum(j + 1, n_tiles - 1), 0, 0), memory_space=pltpu.SMEM)
    row = lambda w: pl.BlockSpec((tmc, w), lambda j: (j, 0))
    if split_rows is None:
        split_tile = None
        out_specs = [row(d), row(d)]
        out_shape = [jax.ShapeDtypeStruct((t, d), F32), jax.ShapeDtypeStruct((t, d), BF16)]
    else:
        split_tile = split_rows // tmc
        out_specs = [pl.BlockSpec((tmc, d), lambda j: (jnp.minimum(j, split_tile - 1), 0)),
                     pl.BlockSpec((tmc, d), lambda j: (jnp.maximum(j - split_tile, 0), 0))]
        out_shape = [jax.ShapeDtypeStruct((split_rows, d), F32), jax.ShapeDtypeStruct((t - split_rows, d), F32)]
    return pl.pallas_call(
        functools.partial(_combine_kernel, alpha=alpha, tmc=tmc, split_tile=split_tile),
        grid=(n_tiles,),
        in_specs=[cur, cur, nxt, nxt, row(d), row(LANES), pl.BlockSpec(memory_space=pl.ANY),
                  pl.BlockSpec((1, d), lambda j: (0, 0)), pl.BlockSpec((1, d), lambda j: (0, 0))],
        out_specs=out_specs,
        out_shape=out_shape,
        scratch_shapes=[pltpu.VMEM((2, 2, tmc, d), F32), pltpu.SemaphoreType.DMA((2,))],
        compiler_params=_params(("arbitrary",)),
        name="moe_combine",
    )(p1, p2, p1, p2, hf, rec, o_sorted, ln_g.reshape(1, d), ln_b.reshape(1, d))


def _pad_to(a, axis, mult):
    pad = (-a.shape[axis]) % mult
    if pad == 0:
        return a
    widths = [(0, 0)] * a.ndim
    widths[axis] = (0, pad)
    return jnp.pad(a, widths)


def kernel(x_prompt, x_sample, ln_emb_g, ln_emb_b, w_in, sink, lambda_q1, lambda_k1, lambda_q2, lambda_k2,
           subln_g, w_pa, w_pd, w_o, ln1_g, ln1_b, ln2_g, ln2_b, ffn_w1, ffn_w3, ffn_w2, router_w,
           moe_w1, moe_w3, moe_w2):
    bp, seq, d = x_prompt.shape
    bs, seq_s, _ = x_sample.shape
    assert seq == seq_s
    depth = w_in.shape[0]
    n_seq = bp + bs
    t = n_seq * seq
    alpha = (2.0 * depth) ** 0.25
    slopes = [2.0 ** (-8.0 * (h + 1) / N_ALIBI_HEADS) for h in range(N_ALIBI_HEADS)]
    win_slopes2 = jnp.asarray([s * LOG2E for s in slopes[:WIN_Q_HEADS]], F32)
    diff_slopes2 = jnp.asarray([s * LOG2E for s in slopes[WIN_Q_HEADS:]], F32)
    col = jnp.arange(QKV_COLS)
    is_q = (col < WIN_Q) | ((col >= DIFF_Q_COL0) & (col < DIFF_Q_COL0 + DIFF_QK))
    col_scale = jnp.where(is_q, Q_SCALE, 1.0).astype(F32).reshape(1, QKV_COLS)
    tm_route = _tile(t, ROW_TILE)

    w_in_b, w_pa_b, w_pd_b, w_o_b = (w.astype(BF16) for w in (w_in, w_pa, w_pd, w_o))
    ffn_w1_b = _pad_to(ffn_w1, 2, 512).astype(BF16)
    ffn_w3_b = _pad_to(ffn_w3, 2, 512).astype(BF16)
    ffn_w2_b = _pad_to(ffn_w2, 1, 512).astype(BF16)
    moe_w1_b, moe_w3_b, moe_w2_b = (w.astype(BF16) for w in (moe_w1, moe_w3, moe_w2))

    xf, xb = _embed_ln(x_prompt.reshape(bp * seq, d), x_sample.reshape(bs * seq, d), ln_emb_g, ln_emb_b)

    outs = None
    for l in range(depth):
        lambda_init = 0.8 - 0.6 * math.exp(-0.3 * l)
        z = _inproj(xb, w_in_b, l, col_scale)
        ya = _win_attn(z, jnp.concatenate([sink[l].astype(F32) * LOG2E, win_slopes2]), n_seq, seq)
        yd = _diff_attn(z, diff_slopes2, lambda_q1, lambda_k1, lambda_q2, lambda_k2, subln_g,
                        n_seq, seq, l, lambda_init)
        merge_args = (xb, ya, yd, xf, w_in_b, w_pa_b, w_pd_b, w_o_b, l, ln1_g[l], ln1_b[l], alpha)
        i = l // 2
        last = l == depth - 1
        if l % 2 == 0:
            hf, hb = _merge(*merge_args)
            xf, xb = _ffn(hb, hf, ffn_w1_b, ffn_w3_b, ffn_w2_b, i, ln2_g[l], ln2_b[l], alpha)
        else:
            rw = _pad_to(router_w[i].astype(F32), 1, LANES)
            r_hi = rw.astype(BF16)
            r_lo = (rw - r_hi.astype(F32)).astype(BF16)
            hf, rec, cnt = _merge(*merge_args, router=(r_hi, r_lo))
            pos1, pos2, src, tile_expert, tile_valid = _moe_plan(rec, cnt, tm_route, ROW_TILE)
            o_sorted = _experts(hf, src, tile_expert, tile_valid, moe_w1_b, moe_w3_b, moe_w2_b, i, ROW_TILE)
            res = _combine(hf, rec, o_sorted, pos1, pos2, ln2_g[l], ln2_b[l], alpha,
                           split_rows=bp * seq if last else None)
            if last:
                outs = res
            else:
                xf, xb = res

    if outs is None:
        outs = (xf[:bp * seq], xf[bp * seq:])
    return (outs[0].reshape(bp, seq, d), outs[1].reshape(bs, seq, d))
```

```python
import functools
import math

import jax
import jax.numpy as jnp
from jax import lax
from jax.experimental import pallas as pl
from jax.experimental.pallas import tpu as pltpu

HEAD_DIM = 128
WIN_Q_HEADS = 8
WIN_KV_HEADS = 2
WIN_REP = WIN_Q_HEADS // WIN_KV_HEADS
WINDOW = 128
BLOCK = 128
DIFF_HEADS = 4
N_ALIBI_HEADS = WIN_Q_HEADS + DIFF_HEADS
WIN_Q = WIN_Q_HEADS * HEAD_DIM
WIN_KV = WIN_KV_HEADS * HEAD_DIM
DIFF_QK = DIFF_HEADS * 2 * HEAD_DIM
DIFF_V = DIFF_HEADS * 2 * HEAD_DIM
QKV_COLS = WIN_Q + 2 * WIN_KV + 2 * DIFF_QK + DIFF_V
DIFF_Q_COL0 = WIN_Q + 2 * WIN_KV
N_EXPERTS = 8
LN_EPS = 1e-5
NEG_INF = -1e30
LOG2E = 1.4426950408889634
Q_SCALE = (HEAD_DIM ** -0.5) * LOG2E
LANES = 128
R_E1, R_E2, R_G1, R_G2, R_RANK1, R_RANK2 = range(6)

VMEM_LIMIT = 56 * 1024 * 1024
F32 = jnp.float32
BF16 = jnp.bfloat16
I32 = jnp.int32

ROW_TILE = 512
FFN_ROW_TILE = 768
COMBINE_TILE = 256


def _params(sem, vmem=VMEM_LIMIT):
    return pltpu.CompilerParams(dimension_semantics=sem, vmem_limit_bytes=vmem)


def _tile(n, want):
    for t in range(min(n, want), 0, -1):
        if n % t == 0 and (t % 8 == 0 or t == n):
            return t
    raise ValueError((n, want))


def _layer_norm(v, g, b):
    mu = jnp.mean(v, axis=-1, keepdims=True)
    xc = v - mu
    var = jnp.mean(xc * xc, axis=-1, keepdims=True)
    return xc * lax.rsqrt(var + LN_EPS) * g + b


def _dot(a, b):
    return jnp.dot(a, b, preferred_element_type=F32)


def _dot_t(a, b):
    return lax.dot_general(a, b, (((1,), (1,)), ((), ())), preferred_element_type=F32)


def _embed_ln_kernel(xp_ref, xs_ref, g_ref, b_ref, of_ref, ob_ref, *, n_prompt_tiles):
    i = pl.program_id(0)

    def emit(x_ref):
        y = _layer_norm(x_ref[...], g_ref[...], b_ref[...])
        of_ref[...] = y
        ob_ref[...] = y.astype(BF16)

    @pl.when(i < n_prompt_tiles)
    def _():
        emit(xp_ref)

    @pl.when(i >= n_prompt_tiles)
    def _():
        emit(xs_ref)


def _embed_ln(xp, xs, g, b):
    tp, d = xp.shape
    ts = xs.shape[0]
    tm = _tile(math.gcd(tp, ts), 512)
    npt, nst = tp // tm, ts // tm
    t = tp + ts
    return pl.pallas_call(
        functools.partial(_embed_ln_kernel, n_prompt_tiles=npt),
        grid=(npt + nst,),
        in_specs=[
            pl.BlockSpec((tm, d), lambda i: (jnp.minimum(i, npt - 1), 0)),
            pl.BlockSpec((tm, d), lambda i: (jnp.maximum(i - npt, 0), 0)),
            pl.BlockSpec((1, d), lambda i: (0, 0)),
            pl.BlockSpec((1, d), lambda i: (0, 0)),
        ],
        out_specs=[
            pl.BlockSpec((tm, d), lambda i: (i, 0)),
            pl.BlockSpec((tm, d), lambda i: (i, 0)),
        ],
        out_shape=[jax.ShapeDtypeStruct((t, d), F32), jax.ShapeDtypeStruct((t, d), BF16)],
        compiler_params=_params(("arbitrary",)),
        name="embed_ln",
    )(xp, xs, g.reshape(1, d), b.reshape(1, d))


def _inproj_kernel(x_ref, w_ref, s_ref, o_ref):
    o_ref[...] = (_dot(x_ref[...], w_ref[...]) * s_ref[...]).astype(o_ref.dtype)


def _inproj(xb, w_in, layer, col_scale):
    t, d = xb.shape
    n = QKV_COLS
    tm = _tile(t, 1024)
    tn = _tile(n, 1536)
    return pl.pallas_call(
        _inproj_kernel,
        grid=(t // tm, n // tn),
        in_specs=[
            pl.BlockSpec((tm, d), lambda i, j: (i, 0)),
            pl.BlockSpec((None, d, tn), lambda i, j: (layer, 0, j)),
            pl.BlockSpec((1, tn), lambda i, j: (0, j)),
        ],
        out_specs=pl.BlockSpec((tm, tn), lambda i, j: (i, j)),
        out_shape=jax.ShapeDtypeStruct((t, n), BF16),
        compiler_params=_params(("parallel", "arbitrary")),
        name="inproj",
    )(xb, w_in, col_scale)


def _lane_tiles(x):
    return [x[:, u * LANES:(u + 1) * LANES] for u in range(x.shape[1] // LANES)]


def _win_attn_kernel(sc_ref, q_ref, k_ref, v_ref, o_ref, t_ref, *, seq):
    g = pl.program_id(1)
    kw = min(3 * BLOCK, seq)
    nb = seq // BLOCK
    row = lax.broadcasted_iota(I32, (BLOCK, kw), 0)
    col = lax.broadcasted_iota(I32, (BLOCK, kw), 1)
    k_starts = [min(max(n * BLOCK - BLOCK, 0), seq - kw) for n in range(nb)]
    offsets = sorted({n * BLOCK - k_starts[n] for n in range(nb)})
    dists = {off: jnp.abs(row - col + off) for off in offsets}
    head_cols = [slice(r * HEAD_DIM, (r + 1) * HEAD_DIM) for r in range(WIN_REP)]
    ms = {}
    for r, cs in enumerate(head_cols):
        sink2 = sc_ref[g * WIN_REP + r]
        slope2 = sc_ref[WIN_Q_HEADS + g * WIN_REP + r]
        bias = {off: jnp.where(d <= WINDOW, slope2 * d.astype(F32), -NEG_INF) for off, d in dists.items()}
        for n in range(nb):
            qs = slice(n * BLOCK, (n + 1) * BLOCK)
            ks = slice(k_starts[n], k_starts[n] + kw)
            t = _dot_t(q_ref[qs, cs], k_ref[ks, :]) - bias[n * BLOCK - k_starts[n]]
            t_ref[r, qs, :] = t
            mx = functools.reduce(jnp.maximum, _lane_tiles(t))
            ms[r, n] = jnp.maximum(jnp.max(mx, axis=-1, keepdims=True), sink2)
    for r, cs in enumerate(head_cols):
        sink2 = sc_ref[g * WIN_REP + r]
        for n in range(nb):
            qs = slice(n * BLOCK, (n + 1) * BLOCK)
            ks = slice(k_starts[n], k_starts[n] + kw)
            m = ms[r, n]
            p = jnp.exp2(t_ref[r, qs, :] - m)
            ls = functools.reduce(jnp.add, _lane_tiles(p))
            denom = jnp.sum(ls, axis=-1, keepdims=True) + jnp.exp2(sink2 - m)
            o_ref[qs, cs] = (_dot(p.astype(BF16), v_ref[ks, :]) / denom).astype(o_ref.dtype)


def _win_attn(z, scalars, n_seq, seq):
    gq = WIN_REP * HEAD_DIM
    kw = min(3 * BLOCK, seq)
    k_blk0 = WIN_Q // HEAD_DIM
    v_blk0 = (WIN_Q + WIN_KV) // HEAD_DIM
    return pl.pallas_call(
        functools.partial(_win_attn_kernel, seq=seq),
        grid=(n_seq, WIN_KV_HEADS),
        in_specs=[
            pl.BlockSpec(memory_space=pltpu.SMEM),
            pl.BlockSpec((seq, gq), lambda b, g: (b, g)),
            pl.BlockSpec((seq, HEAD_DIM), lambda b, g: (b, k_blk0 + g)),
            pl.BlockSpec((seq, HEAD_DIM), lambda b, g: (b, v_blk0 + g)),
        ],
        out_specs=pl.BlockSpec((seq, gq), lambda b, g: (b, g)),
        out_shape=jax.ShapeDtypeStruct((n_seq * seq, WIN_Q), BF16),
        scratch_shapes=[pltpu.VMEM((WIN_REP, seq, kw), F32)],
        compiler_params=_params(("parallel", "arbitrary")),
        name="win_attn",
    )(scalars, z, z, z)


def _diff_attn_kernel(sl_ref, lq1_ref, lk1_ref, lq2_ref, lk2_ref, g_ref, q_ref, k_ref, v_ref, o_ref,
                      bias_ref, t_ref, *, seq, tq, rb_rows, kc, layer, lambda_init):
    h = pl.program_id(0)
    qi = pl.program_id(1)
    b = pl.program_id(2)
    n_kc = seq // kc

    @pl.when(b == 0)
    def _():
        tpos = qi * tq + lax.broadcasted_iota(I32, (tq, seq), 0)
        spos = lax.broadcasted_iota(I32, (tq, seq), 1)
        bias_ref[...] = sl_ref[h] * jnp.abs(tpos - spos).astype(F32)

    lam = (jnp.exp(jnp.sum(lq1_ref[layer:layer + 1, :] * lk1_ref[layer:layer + 1, :], axis=-1, keepdims=True))
           - jnp.exp(jnp.sum(lq2_ref[layer:layer + 1, :] * lk2_ref[layer:layer + 1, :], axis=-1, keepdims=True))
           + lambda_init)

    sections = [(rb, c) for rb in range(tq // rb_rows) for c in range(2)]
    key_tiles = [slice(j * kc, (j + 1) * kc) for j in range(n_kc)]
    ms = {}
    for rb, c in sections:
        rs = slice(rb * rb_rows, (rb + 1) * rb_rows)
        cs = slice(c * HEAD_DIM, (c + 1) * HEAD_DIM)
        q = q_ref[rs, cs]
        mx = jnp.full((rb_rows, LANES), -jnp.inf, F32)
        for ks in key_tiles:
            t = _dot_t(q, k_ref[ks, cs]) - bias_ref[rs, ks]
            t_ref[c, rs, ks] = t
            mx = functools.reduce(jnp.maximum, _lane_tiles(t), mx)
        ms[rb, c] = jnp.max(mx, axis=-1, keepdims=True)
    outs = {}
    for rb, c in sections:
        rs = slice(rb * rb_rows, (rb + 1) * rb_rows)
        ls = jnp.zeros((rb_rows, LANES), F32)
        acc = jnp.zeros((rb_rows, 2 * HEAD_DIM), F32)
        for ks in key_tiles:
            p = jnp.exp2(t_ref[c, rs, ks] - ms[rb, c])
            ls = functools.reduce(jnp.add, _lane_tiles(p), ls)
            acc = acc + _dot(p.astype(BF16), v_ref[ks, :])
        outs[rb, c] = acc / jnp.sum(ls, axis=-1, keepdims=True)
    for rb in range(tq // rb_rows):
        rs = slice(rb * rb_rows, (rb + 1) * rb_rows)
        o = outs[rb, 0] - lam * outs[rb, 1]
        rms = lax.rsqrt(jnp.mean(o * o, axis=-1, keepdims=True) + LN_EPS)
        o_ref[rs, :] = (o * rms * g_ref[layer:layer + 1, :] * (1.0 - lambda_init)).astype(o_ref.dtype)


def _diff_attn(z, slopes2, lq1, lk1, lq2, lk2, subln_g, n_seq, seq, layer, lambda_init):
    hw = 2 * HEAD_DIM
    tq = _tile(seq, 1024)
    rb_rows = _tile(tq, 128)
    kc = _tile(seq, 256)
    nq = seq // tq
    q_blk0 = DIFF_Q_COL0 // hw
    k_blk0 = q_blk0 + DIFF_QK // hw
    v_blk0 = k_blk0 + DIFF_QK // hw
    whole = lambda a: pl.BlockSpec(a.shape, lambda h, qi, b: (0, 0))
    return pl.pallas_call(
        functools.partial(_diff_attn_kernel, seq=seq, tq=tq, rb_rows=rb_rows, kc=kc, layer=layer,
                          lambda_init=lambda_init),
        grid=(DIFF_HEADS, nq, n_seq),
        in_specs=[
            pl.BlockSpec(memory_space=pltpu.SMEM),
            whole(lq1), whole(lk1), whole(lq2), whole(lk2), whole(subln_g),
            pl.BlockSpec((tq, hw), lambda h, qi, b: (b * nq + qi, q_blk0 + h)),
            pl.BlockSpec((seq, hw), lambda h, qi, b: (b, k_blk0 + h)),
            pl.BlockSpec((seq, hw), lambda h, qi, b: (b, v_blk0 + h)),
        ],
        out_specs=pl.BlockSpec((tq, hw), lambda h, qi, b: (b * nq + qi, h)),
        out_shape=jax.ShapeDtypeStruct((n_seq * seq, DIFF_V), BF16),
        scratch_shapes=[pltpu.VMEM((tq, seq), F32), pltpu.VMEM((2, tq, seq), F32)],
        compiler_params=_params(("arbitrary", "arbitrary", "arbitrary")),
        name="diff_attn",
    )(slopes2, lq1, lk1, lq2, lk2, subln_g, z, z, z)


def _residual_copy(x_hbm, buf, sem, tm):
    row0 = pl.multiple_of(pl.program_id(0) * tm, tm)
    return pltpu.make_async_copy(x_hbm.at[pl.ds(row0, tm), :], buf, sem)


def _route(hf, rhi_ref, rlo_ref):
    tm = hf.shape[0]
    h_hi = hf.astype(BF16)
    h_lo = (hf - h_hi.astype(F32)).astype(BF16)
    logits = _dot(h_hi, rhi_ref[...]) + _dot(h_lo, rhi_ref[...]) + _dot(h_hi, rlo_ref[...])
    lane = lax.broadcasted_iota(I32, logits.shape, 1)
    lg = jnp.where(lane < N_EXPERTS, logits, -jnp.inf)
    m1 = jnp.max(lg, axis=-1, keepdims=True)
    i1 = jnp.min(jnp.where(lg == m1, lane, LANES), axis=-1, keepdims=True)
    lg2 = jnp.where(lane == i1, -jnp.inf, lg)
    m2 = jnp.max(lg2, axis=-1, keepdims=True)
    i2 = jnp.min(jnp.where(lg2 == m2, lane, LANES), axis=-1, keepdims=True)
    e2 = jnp.exp(m2 - m1)
    g1 = 1.0 / (1.0 + e2)
    g2 = e2 / (1.0 + e2)
    oh1 = lane == i1
    oh2 = lane == i2
    sel = jnp.logical_or(oh1, oh2).astype(F32)
    r_idx = lax.broadcasted_iota(I32, (tm, tm), 0)
    c_idx = lax.broadcasted_iota(I32, (tm, tm), 1)
    lower = (c_idx < r_idx).astype(F32).astype(BF16)
    before = _dot(lower, sel.astype(BF16))
    rank1 = jnp.sum(jnp.where(oh1, before, 0.0), axis=-1, keepdims=True)
    rank2 = jnp.sum(jnp.where(oh2, before, 0.0), axis=-1, keepdims=True)
    counts = jnp.sum(sel, axis=0, keepdims=True)
    rec = jnp.zeros(logits.shape, F32)
    for ln, val in ((R_E1, i1.astype(F32)), (R_E2, i2.astype(F32)), (R_G1, g1), (R_G2, g2),
                    (R_RANK1, rank1), (R_RANK2, rank2)):
        rec = jnp.where(lane == ln, val, rec)
    return rec, counts


def _merge_kernel(*refs, alpha, with_router, tm):
    if with_router:
        (xb_ref, ya_ref, yd_ref, xf_hbm, wga_ref, wgd_ref, wpa_ref, wpd_ref, wo_ref, g_ref, b_ref,
         rhi_ref, rlo_ref, hf_ref, rec_ref, cnt_ref, xf_buf, xf_sem) = refs
    else:
        (xb_ref, ya_ref, yd_ref, xf_hbm, wga_ref, wgd_ref, wpa_ref, wpd_ref, wo_ref, g_ref, b_ref,
         hf_ref, hb_ref, xf_buf, xf_sem) = refs
    j = pl.program_id(1)

    @pl.when(j == 0)
    def _():
        _residual_copy(xf_hbm, xf_buf, xf_sem, tm).start()
        hf_ref[...] = jnp.zeros(hf_ref.shape, F32)

    x = xb_ref[...]
    ga = _dot(x, wga_ref[...])
    gd = _dot(x, wgd_ref[...])
    pa = _dot(ya_ref[...], wpa_ref[...])
    pd = _dot(yd_ref[...], wpd_ref[...])
    merged = (jax.nn.sigmoid(ga) * pa + jax.nn.sigmoid(gd) * pd).astype(BF16)
    hf_ref[...] += _dot(merged, wo_ref[...])

    @pl.when(j == pl.num_programs(1) - 1)
    def _():
        _residual_copy(xf_hbm, xf_buf, xf_sem, tm).wait()
        hf = _layer_norm(alpha * xf_buf[...] + hf_ref[...], g_ref[...], b_ref[...])
        hf_ref[...] = hf
        if with_router:
            rec, counts = _route(hf, rhi_ref, rlo_ref)
            rec_ref[...] = rec
            cnt_ref[...] = jnp.broadcast_to(counts, cnt_ref.shape)
        else:
            hb_ref[...] = hf.astype(BF16)


def _merge(xb, ya, yd, xf, w_in, w_pa, w_pd, w_o, layer, ln_g, ln_b, alpha, router=None):
    t, d = xb.shape
    tm = _tile(t, ROW_TILE)
    tn = _tile(d, 512)
    nj = d // tn
    ga_blk0 = QKV_COLS // tn
    with_router = router is not None
    row = lambda w: pl.BlockSpec((tm, w), lambda i, j: (i, 0))
    in_specs = [
        row(d), row(WIN_Q), row(DIFF_V),
        pl.BlockSpec(memory_space=pl.ANY),
        pl.BlockSpec((None, d, tn), lambda i, j: (layer, 0, ga_blk0 + j)),
        pl.BlockSpec((None, d, tn), lambda i, j: (layer, 0, ga_blk0 + nj + j)),
        pl.BlockSpec((None, WIN_Q, tn), lambda i, j: (layer, 0, j)),
        pl.BlockSpec((None, DIFF_V, tn), lambda i, j: (layer, 0, j)),
        pl.BlockSpec((None, tn, d), lambda i, j: (layer, j, 0)),
        pl.BlockSpec((1, d), lambda i, j: (0, 0)),
        pl.BlockSpec((1, d), lambda i, j: (0, 0)),
    ]
    args = [xb, ya, yd, xf, w_in, w_in, w_pa, w_pd, w_o, ln_g.reshape(1, d), ln_b.reshape(1, d)]
    out_specs = [row(d)]
    out_shape = [jax.ShapeDtypeStruct((t, d), F32)]
    if with_router:
        in_specs += [pl.BlockSpec((d, LANES), lambda i, j: (0, 0))] * 2
        args += list(router)
        out_specs += [row(LANES), pl.BlockSpec((8, LANES), lambda i, j: (i, 0))]
        out_shape += [jax.ShapeDtypeStruct((t, LANES), F32), jax.ShapeDtypeStruct((t // tm * 8, LANES), F32)]
    else:
        out_specs.append(row(d))
        out_shape.append(jax.ShapeDtypeStruct((t, d), BF16))
    return pl.pallas_call(
        functools.partial(_merge_kernel, alpha=alpha, with_router=with_router, tm=tm),
        grid=(t // tm, nj),
        in_specs=in_specs,
        out_specs=out_specs,
        out_shape=out_shape,
        scratch_shapes=[pltpu.VMEM((tm, d), F32), pltpu.SemaphoreType.DMA(())],
        compiler_params=_params(("arbitrary", "arbitrary")),
        name="merge_router" if with_router else "merge",
    )(*args)


def _swiglu(x, w1, w3):
    u = _dot(x, w1)
    return (u * jax.nn.sigmoid(u) * _dot(x, w3)).astype(BF16)


def _ffn_kernel(hb_ref, hf_hbm, w1_ref, w3_ref, w2_ref, g_ref, b_ref, of_ref, ob_ref, hf_buf, hf_sem, *, alpha, tm):
    f = pl.program_id(1)

    @pl.when(f == 0)
    def _():
        _residual_copy(hf_hbm, hf_buf, hf_sem, tm).start()
        of_ref[...] = jnp.zeros(of_ref.shape, F32)

    of_ref[...] += _dot(_swiglu(hb_ref[...], w1_ref[...], w3_ref[...]), w2_ref[...])

    @pl.when(f == pl.num_programs(1) - 1)
    def _():
        _residual_copy(hf_hbm, hf_buf, hf_sem, tm).wait()
        y = _layer_norm(alpha * hf_buf[...] + of_ref[...], g_ref[...], b_ref[...])
        of_ref[...] = y
        ob_ref[...] = y.astype(BF16)


def _ffn(hb, hf, w1, w3, w2, idx, ln_g, ln_b, alpha):
    t, d = hb.shape
    ff = w1.shape[2]
    tm = _tile(t, FFN_ROW_TILE)
    tf = _tile(ff, 512)
    row = pl.BlockSpec((tm, d), lambda i, f: (i, 0))
    return pl.pallas_call(
        functools.partial(_ffn_kernel, alpha=alpha, tm=tm),
        grid=(t // tm, ff // tf),
        in_specs=[
            row,
            pl.BlockSpec(memory_space=pl.ANY),
            pl.BlockSpec((None, d, tf), lambda i, f: (idx, 0, f)),
            pl.BlockSpec((None, d, tf), lambda i, f: (idx, 0, f)),
            pl.BlockSpec((None, tf, d), lambda i, f: (idx, f, 0)),
            pl.BlockSpec((1, d), lambda i, f: (0, 0)),
            pl.BlockSpec((1, d), lambda i, f: (0, 0)),
        ],
        out_specs=[row, row],
        out_shape=[jax.ShapeDtypeStruct((t, d), F32), jax.ShapeDtypeStruct((t, d), BF16)],
        scratch_shapes=[pltpu.VMEM((tm, d), F32), pltpu.SemaphoreType.DMA(())],
        compiler_params=_params(("arbitrary", "arbitrary")),
        name="ffn",
    )(hb, hf, w1, w3, w2, ln_g.reshape(1, d), ln_b.reshape(1, d))


def _moe_plan(rec, cnt, tm_route, tmx):
    t = rec.shape[0]
    e1 = rec[:, R_E1].astype(I32)
    e2 = rec[:, R_E2].astype(I32)
    rank1 = rec[:, R_RANK1].astype(I32)
    rank2 = rec[:, R_RANK2].astype(I32)
    tile_cnt = cnt.reshape(t // tm_route, 8, LANES)[:, 0, :N_EXPERTS].astype(I32)
    counts = jnp.sum(tile_cnt, axis=0)
    tile_base = jnp.cumsum(tile_cnt, axis=0) - tile_cnt
    pad_cnt = (counts + tmx - 1) // tmx * tmx
    pad_end = jnp.cumsum(pad_cnt)
    pad_start = pad_end - pad_cnt
    base = jnp.repeat(tile_base + pad_start[None, :], tm_route, axis=0)
    eids = jnp.arange(N_EXPERTS, dtype=I32)[None, :]
    pos1 = jnp.sum(jnp.where(e1[:, None] == eids, base, 0), axis=-1) + rank1
    pos2 = jnp.sum(jnp.where(e2[:, None] == eids, base, 0), axis=-1) + rank2
    n_rows = 2 * t + N_EXPERTS * tmx
    jj = jnp.arange(tmx, dtype=I32)[None, :]
    dummy_pos = jnp.where(jj < (pad_cnt - counts)[:, None], (pad_start + counts)[:, None] + jj, n_rows)
    tok = jnp.arange(t, dtype=I32)
    keys = jnp.concatenate([pos1, pos2, dummy_pos.reshape(-1)])
    vals = jnp.concatenate([tok, tok, jnp.zeros((N_EXPERTS * tmx,), I32)])
    _, src = lax.sort((keys, vals), num_keys=1)
    n_tiles = n_rows // tmx
    tile_start = jnp.arange(n_tiles, dtype=I32) * tmx
    tile_expert = jnp.minimum(jnp.sum((tile_start[:, None] >= pad_end[None, :]).astype(I32), axis=-1), N_EXPERTS - 1)
    tile_valid = (tile_start < pad_end[-1]).astype(I32)
    return pos1, pos2, src, tile_expert, tile_valid


def _row_gather(idx_ref, src_hbm, dst, sem, n):
    for r in range(n):
        pltpu.make_async_copy(src_hbm.at[pl.ds(idx_ref[0, 0, r], 1), :], dst.at[pl.ds(r, 1), :], sem).start()


def _rows_wait(src_hbm, dst, sem, n):
    pltpu.make_async_copy(src_hbm.at[pl.ds(0, n), :], dst, sem).wait()


def _experts_kernel(te_ref, tv_ref, cur_ref, nxt_ref, x_hbm, w1_ref, w3_ref, w2_ref, o_ref, xbuf, xb_ref, sems, *,
                    tmx):
    j = pl.program_id(0)
    n = pl.num_programs(0)
    slot = j % 2
    valid = tv_ref[j] > 0
    prev_valid = tv_ref[jnp.maximum(j - 1, 0)] > 0

    @pl.when(j == 0)
    def _():
        _row_gather(cur_ref, x_hbm, xbuf.at[0], sems.at[0], tmx)

    @pl.when(valid)
    def _():
        _row_gather(nxt_ref, x_hbm, xbuf.at[1 - slot], sems.at[1 - slot], tmx)
        _rows_wait(x_hbm, xbuf.at[slot], sems.at[slot], tmx)
        xb_ref[...] = xbuf[slot].astype(BF16)
        o_ref[...] = _dot(_swiglu(xb_ref[...], w1_ref[...], w3_ref[...]), w2_ref[...])

    @pl.when(jnp.logical_not(valid))
    def _():
        o_ref[...] = jnp.zeros(o_ref.shape, F32)

    @pl.when(jnp.logical_and(jnp.logical_not(valid), jnp.logical_and(prev_valid, j > 0)))
    def _():
        _rows_wait(x_hbm, xbuf.at[slot], sems.at[slot], tmx)

    @pl.when(jnp.logical_and(valid, j == n - 1))
    def _():
        _rows_wait(x_hbm, xbuf.at[1 - slot], sems.at[1 - slot], tmx)


def _experts(hf, src, tile_expert, tile_valid, w1, w3, w2, idx, tmx):
    t, d = hf.shape
    ff = w1.shape[3]
    n_tiles = src.shape[0] // tmx
    src3 = src.reshape(n_tiles, 1, tmx)
    grid_spec = pltpu.PrefetchScalarGridSpec(
        num_scalar_prefetch=2,
        grid=(n_tiles,),
        in_specs=[
            pl.BlockSpec((1, 1, tmx), lambda j, te, tv: (j, 0, 0), memory_space=pltpu.SMEM),
            pl.BlockSpec((1, 1, tmx), lambda j, te, tv: (jnp.minimum(j + 1, n_tiles - 1), 0, 0),
                         memory_space=pltpu.SMEM),
            pl.BlockSpec(memory_space=pl.ANY),
            pl.BlockSpec((None, None, d, ff), lambda j, te, tv: (idx, te[j], 0, 0)),
            pl.BlockSpec((None, None, d, ff), lambda j, te, tv: (idx, te[j], 0, 0)),
            pl.BlockSpec((None, None, ff, d), lambda j, te, tv: (idx, te[j], 0, 0)),
        ],
        out_specs=pl.BlockSpec((tmx, d), lambda j, te, tv: (j, 0)),
        scratch_shapes=[pltpu.VMEM((2, tmx, d), F32), pltpu.VMEM((tmx, d), BF16), pltpu.SemaphoreType.DMA((2,))],
    )
    return pl.pallas_call(
        functools.partial(_experts_kernel, tmx=tmx),
        grid_spec=grid_spec,
        out_shape=jax.ShapeDtypeStruct((n_tiles * tmx, d), F32),
        compiler_params=_params(("arbitrary",)),
        name="moe_experts",
    )(tile_expert, tile_valid, src3, src3, hf, w1, w3, w2)


def _combine_kernel(p1c_ref, p2c_ref, p1n_ref, p2n_ref, hf_ref, rec_ref, o_hbm, g_ref, b_ref, out0_ref, out1_ref,
                    obuf, sems, *, alpha, tmc, split_tile):
    j = pl.program_id(0)
    n = pl.num_programs(0)
    slot = j % 2

    def gather(p1_ref, p2_ref, s):
        _row_gather(p1_ref, o_hbm, obuf.at[s, 0], sems.at[s], tmc)
        _row_gather(p2_ref, o_hbm, obuf.at[s, 1], sems.at[s], tmc)

    def wait(s):
        _rows_wait(o_hbm, obuf.at[s, 0], sems.at[s], tmc)
        _rows_wait(o_hbm, obuf.at[s, 1], sems.at[s], tmc)

    @pl.when(j == 0)
    def _():
        gather(p1c_ref, p2c_ref, 0)

    gather(p1n_ref, p2n_ref, 1 - slot)
    wait(slot)
    rec = rec_ref[...]
    g1 = rec[:, R_G1:R_G1 + 1]
    g2 = rec[:, R_G2:R_G2 + 1]
    y = _layer_norm(alpha * hf_ref[...] + g1 * obuf[slot, 0] + g2 * obuf[slot, 1], g_ref[...], b_ref[...])
    if split_tile is None:
        out0_ref[...] = y
        out1_ref[...] = y.astype(BF16)
    else:
        @pl.when(j < split_tile)
        def _():
            out0_ref[...] = y

        @pl.when(j >= split_tile)
        def _():
            out1_ref[...] = y

    @pl.when(j == n - 1)
    def _():
        wait(1 - slot)


def _combine(hf, rec, o_sorted, pos1, pos2, ln_g, ln_b, alpha, split_rows=None):
    t, d = hf.shape
    tmc = _tile(t if split_rows is None else math.gcd(t, split_rows), COMBINE_TILE)
    n_tiles = t // tmc
    p1 = pos1.reshape(n_tiles, 1, tmc)
    p2 = pos2.reshape(n_tiles, 1, tmc)
    cur = pl.BlockSpec((1, 1, tmc), lambda j: (j, 0, 0), memory_space=pltpu.SMEM)
    nxt = pl.BlockSpec((1, 1, tmc), lambda j: (jnp.minimum(j + 1, n_tiles - 1), 0, 0), memory_space=pltpu.SMEM)
    row = lambda w: pl.BlockSpec((tmc, w), lambda j: (j, 0))
    if split_rows is None:
        split_tile = None
        out_specs = [row(d), row(d)]
        out_shape = [jax.ShapeDtypeStruct((t, d), F32), jax.ShapeDtypeStruct((t, d), BF16)]
    else:
        split_tile = split_rows // tmc
        out_specs = [pl.BlockSpec((tmc, d), lambda j: (jnp.minimum(j, split_tile - 1), 0)),
                     pl.BlockSpec((tmc, d), lambda j: (jnp.maximum(j - split_tile, 0), 0))]
        out_shape = [jax.ShapeDtypeStruct((split_rows, d), F32), jax.ShapeDtypeStruct((t - split_rows, d), F32)]
    return pl.pallas_call(
        functools.partial(_combine_kernel, alpha=alpha, tmc=tmc, split_tile=split_tile),
        grid=(n_tiles,),
        in_specs=[cur, cur, nxt, nxt, row(d), row(LANES), pl.BlockSpec(memory_space=pl.ANY),
                  pl.BlockSpec((1, d), lambda j: (0, 0)), pl.BlockSpec((1, d), lambda j: (0, 0))],
        out_specs=out_specs,
        out_shape=out_shape,
        scratch_shapes=[pltpu.VMEM((2, 2, tmc, d), F32), pltpu.SemaphoreType.DMA((2,))],
        compiler_params=_params(("arbitrary",)),
        name="moe_combine",
    )(p1, p2, p1, p2, hf, rec, o_sorted, ln_g.reshape(1, d), ln_b.reshape(1, d))


def _pad_to(a, axis, mult):
    pad = (-a.shape[axis]) % mult
    if pad == 0:
        return a
    widths = [(0, 0)] * a.ndim
    widths[axis] = (0, pad)
    return jnp.pad(a, widths)


def kernel(x_prompt, x_sample, ln_emb_g, ln_emb_b, w_in, sink, lambda_q1, lambda_k1, lambda_q2, lambda_k2,
           subln_g, w_pa, w_pd, w_o, ln1_g, ln1_b, ln2_g, ln2_b, ffn_w1, ffn_w3, ffn_w2, router_w,
           moe_w1, moe_w3, moe_w2):
    bp, seq, d = x_prompt.shape
    bs, seq_s, _ = x_sample.shape
    assert seq == seq_s
    depth = w_in.shape[0]
    n_seq = bp + bs
    t = n_seq * seq
    alpha = (2.0 * depth) ** 0.25
    slopes = [2.0 ** (-8.0 * (h + 1) / N_ALIBI_HEADS) for h in range(N_ALIBI_HEADS)]
    win_slopes2 = jnp.asarray([s * LOG2E for s in slopes[:WIN_Q_HEADS]], F32)
    diff_slopes2 = jnp.asarray([s * LOG2E for s in slopes[WIN_Q_HEADS:]], F32)
    col = jnp.arange(QKV_COLS)
    is_q = (col < WIN_Q) | ((col >= DIFF_Q_COL0) & (col < DIFF_Q_COL0 + DIFF_QK))
    col_scale = jnp.where(is_q, Q_SCALE, 1.0).astype(F32).reshape(1, QKV_COLS)
    tm_route = _tile(t, ROW_TILE)

    w_in_b, w_pa_b, w_pd_b, w_o_b = (w.astype(BF16) for w in (w_in, w_pa, w_pd, w_o))
    ffn_w1_b = _pad_to(ffn_w1.astype(BF16), 2, 512)
    ffn_w3_b = _pad_to(ffn_w3.astype(BF16), 2, 512)
    ffn_w2_b = _pad_to(ffn_w2.astype(BF16), 1, 512)
    moe_w1_b, moe_w3_b, moe_w2_b = (w.astype(BF16) for w in (moe_w1, moe_w3, moe_w2))

    xf, xb = _embed_ln(x_prompt.reshape(bp * seq, d), x_sample.reshape(bs * seq, d), ln_emb_g, ln_emb_b)

    outs = None
    for l in range(depth):
        lambda_init = 0.8 - 0.6 * math.exp(-0.3 * l)
        z = _inproj(xb, w_in_b, l, col_scale)
        ya = _win_attn(z, jnp.concatenate([sink[l].astype(F32) * LOG2E, win_slopes2]), n_seq, seq)
        yd = _diff_attn(z, diff_slopes2, lambda_q1, lambda_k1, lambda_q2, lambda_k2, subln_g,
                        n_seq, seq, l, lambda_init)
        merge_args = (xb, ya, yd, xf, w_in_b, w_pa_b, w_pd_b, w_o_b, l, ln1_g[l], ln1_b[l], alpha)
        i = l // 2
        last = l == depth - 1
        if l % 2 == 0:
            hf, hb = _merge(*merge_args)
            xf, xb = _ffn(hb, hf, ffn_w1_b, ffn_w3_b, ffn_w2_b, i, ln2_g[l], ln2_b[l], alpha)
        else:
            rw = _pad_to(router_w[i].astype(F32), 1, LANES)
            r_hi = rw.astype(BF16)
            r_lo = (rw - r_hi.astype(F32)).astype(BF16)
            hf, rec, cnt = _merge(*merge_args, router=(r_hi, r_lo))
            pos1, pos2, src, tile_expert, tile_valid = _moe_plan(rec, cnt, tm_route, ROW_TILE)
            o_sorted = _experts(hf, src, tile_expert, tile_valid, moe_w1_b, moe_w3_b, moe_w2_b, i, ROW_TILE)
            res = _combine(hf, rec, o_sorted, pos1, pos2, ln2_g[l], ln2_b[l], alpha,
                           split_rows=bp * seq if last else None)
            if last:
                outs = res
            else:
                xf, xb = res

    if outs is None:
        outs = (xf[:bp * seq], xf[bp * seq:])
    return (outs[0].reshape(bp, seq, d), outs[1].reshape(bs, seq, d))
```

```python
import functools
import math

import jax
import jax.numpy as jnp
from jax import lax
from jax.experimental import pallas as pl
from jax.experimental.pallas import tpu as pltpu

HEAD_DIM = 128
WIN_Q_HEADS = 8
WIN_KV_HEADS = 2
WIN_REP = WIN_Q_HEADS // WIN_KV_HEADS
WINDOW = 128
BLOCK = 128
DIFF_HEADS = 4
N_ALIBI_HEADS = WIN_Q_HEADS + DIFF_HEADS
WIN_Q = WIN_Q_HEADS * HEAD_DIM
WIN_KV = WIN_KV_HEADS * HEAD_DIM
DIFF_QK = DIFF_HEADS * 2 * HEAD_DIM
DIFF_V = DIFF_HEADS * 2 * HEAD_DIM
QKV_COLS = WIN_Q + 2 * WIN_KV + 2 * DIFF_QK + DIFF_V
DIFF_Q_COL0 = WIN_Q + 2 * WIN_KV
N_EXPERTS = 8
LN_EPS = 1e-5
NEG_INF = -1e30
LOG2E = 1.4426950408889634
Q_SCALE = (HEAD_DIM ** -0.5) * LOG2E
LANES = 128
R_E1, R_E2, R_G1, R_G2, R_RANK1, R_RANK2 = range(6)

VMEM_LIMIT = 56 * 1024 * 1024
F32 = jnp.float32
BF16 = jnp.bfloat16
I32 = jnp.int32

ROW_TILE = 512
MERGE_ROW_TILE = 256
FFN_ROW_TILE = 768
COMBINE_TILE = 256


def _params(sem, vmem=VMEM_LIMIT):
    return pltpu.CompilerParams(dimension_semantics=sem, vmem_limit_bytes=vmem)


def _tile(n, want):
    for t in range(min(n, want), 0, -1):
        if n % t == 0 and (t % 8 == 0 or t == n):
            return t
    raise ValueError((n, want))


def _layer_norm(v, g, b):
    mu = jnp.mean(v, axis=-1, keepdims=True)
    xc = v - mu
    var = jnp.mean(xc * xc, axis=-1, keepdims=True)
    return xc * lax.rsqrt(var + LN_EPS) * g + b


def _dot(a, b):
    return jnp.dot(a, b, preferred_element_type=F32)


def _dot_t(a, b):
    return lax.dot_general(a, b, (((1,), (1,)), ((), ())), preferred_element_type=F32)


def _embed_ln_kernel(xp_ref, xs_ref, g_ref, b_ref, of_ref, ob_ref, *, n_prompt_tiles):
    i = pl.program_id(0)

    def emit(x_ref):
        y = _layer_norm(x_ref[...], g_ref[...], b_ref[...])
        of_ref[...] = y
        ob_ref[...] = y.astype(BF16)

    @pl.when(i < n_prompt_tiles)
    def _():
        emit(xp_ref)

    @pl.when(i >= n_prompt_tiles)
    def _():
        emit(xs_ref)


def _embed_ln(xp, xs, g, b):
    tp, d = xp.shape
    ts = xs.shape[0]
    tm = _tile(math.gcd(tp, ts), 512)
    npt, nst = tp // tm, ts // tm
    t = tp + ts
    return pl.pallas_call(
        functools.partial(_embed_ln_kernel, n_prompt_tiles=npt),
        grid=(npt + nst,),
        in_specs=[
            pl.BlockSpec((tm, d), lambda i: (jnp.minimum(i, npt - 1), 0)),
            pl.BlockSpec((tm, d), lambda i: (jnp.maximum(i - npt, 0), 0)),
            pl.BlockSpec((1, d), lambda i: (0, 0)),
            pl.BlockSpec((1, d), lambda i: (0, 0)),
        ],
        out_specs=[
            pl.BlockSpec((tm, d), lambda i: (i, 0)),
            pl.BlockSpec((tm, d), lambda i: (i, 0)),
        ],
        out_shape=[jax.ShapeDtypeStruct((t, d), F32), jax.ShapeDtypeStruct((t, d), BF16)],
        compiler_params=_params(("arbitrary",)),
        name="embed_ln",
    )(xp, xs, g.reshape(1, d), b.reshape(1, d))


def _inproj_kernel(x_ref, w_ref, s_ref, o_ref):
    o_ref[...] = (_dot(x_ref[...], w_ref[...]) * s_ref[...]).astype(o_ref.dtype)


def _inproj(xb, w_in, layer, col_scale):
    t, d = xb.shape
    n = QKV_COLS
    tm = _tile(t, 1024)
    tn = _tile(n, 1536)
    return pl.pallas_call(
        _inproj_kernel,
        grid=(t // tm, n // tn),
        in_specs=[
            pl.BlockSpec((tm, d), lambda i, j: (i, 0)),
            pl.BlockSpec((None, d, tn), lambda i, j: (layer, 0, j)),
            pl.BlockSpec((1, tn), lambda i, j: (0, j)),
        ],
        out_specs=pl.BlockSpec((tm, tn), lambda i, j: (i, j)),
        out_shape=jax.ShapeDtypeStruct((t, n), BF16),
        compiler_params=_params(("parallel", "arbitrary")),
        name="inproj",
    )(xb, w_in, col_scale)


def _lane_tiles(x):
    return [x[:, u * LANES:(u + 1) * LANES] for u in range(x.shape[1] // LANES)]


def _win_attn_kernel(sc_ref, q_ref, k_ref, v_ref, o_ref, t_ref, *, seq):
    g = pl.program_id(1)
    kw = min(3 * BLOCK, seq)
    nb = seq // BLOCK
    row = lax.broadcasted_iota(I32, (BLOCK, kw), 0)
    col = lax.broadcasted_iota(I32, (BLOCK, kw), 1)
    k_starts = [min(max(n * BLOCK - BLOCK, 0), seq - kw) for n in range(nb)]
    offsets = sorted({n * BLOCK - k_starts[n] for n in range(nb)})
    dists = {off: jnp.abs(row - col + off) for off in offsets}
    head_cols = [slice(r * HEAD_DIM, (r + 1) * HEAD_DIM) for r in range(WIN_REP)]
    ms = {}
    for r, cs in enumerate(head_cols):
        sink2 = sc_ref[g * WIN_REP + r]
        slope2 = sc_ref[WIN_Q_HEADS + g * WIN_REP + r]
        bias = {off: jnp.where(d <= WINDOW, slope2 * d.astype(F32), -NEG_INF) for off, d in dists.items()}
        for n in range(nb):
            qs = slice(n * BLOCK, (n + 1) * BLOCK)
            ks = slice(k_starts[n], k_starts[n] + kw)
            t = _dot_t(q_ref[qs, cs], k_ref[ks, :]) - bias[n * BLOCK - k_starts[n]]
            t_ref[r, qs, :] = t
            mx = functools.reduce(jnp.maximum, _lane_tiles(t))
            ms[r, n] = jnp.maximum(jnp.max(mx, axis=-1, keepdims=True), sink2)
    for r, cs in enumerate(head_cols):
        sink2 = sc_ref[g * WIN_REP + r]
        for n in range(nb):
            qs = slice(n * BLOCK, (n + 1) * BLOCK)
            ks = slice(k_starts[n], k_starts[n] + kw)
            m = ms[r, n]
            p = jnp.exp2(t_ref[r, qs, :] - m)
            ls = functools.reduce(jnp.add, _lane_tiles(p))
            denom = jnp.sum(ls, axis=-1, keepdims=True) + jnp.exp2(sink2 - m)
            o_ref[qs, cs] = (_dot(p.astype(BF16), v_ref[ks, :]) / denom).astype(o_ref.dtype)


def _win_attn(z, scalars, n_seq, seq):
    gq = WIN_REP * HEAD_DIM
    kw = min(3 * BLOCK, seq)
    k_blk0 = WIN_Q // HEAD_DIM
    v_blk0 = (WIN_Q + WIN_KV) // HEAD_DIM
    return pl.pallas_call(
        functools.partial(_win_attn_kernel, seq=seq),
        grid=(n_seq, WIN_KV_HEADS),
        in_specs=[
            pl.BlockSpec(memory_space=pltpu.SMEM),
            pl.BlockSpec((seq, gq), lambda b, g: (b, g)),
            pl.BlockSpec((seq, HEAD_DIM), lambda b, g: (b, k_blk0 + g)),
            pl.BlockSpec((seq, HEAD_DIM), lambda b, g: (b, v_blk0 + g)),
        ],
        out_specs=pl.BlockSpec((seq, gq), lambda b, g: (b, g)),
        out_shape=jax.ShapeDtypeStruct((n_seq * seq, WIN_Q), BF16),
        scratch_shapes=[pltpu.VMEM((WIN_REP, seq, kw), F32)],
        compiler_params=_params(("parallel", "arbitrary")),
        name="win_attn",
    )(scalars, z, z, z)


def _diff_attn_kernel(sl_ref, lq1_ref, lk1_ref, lq2_ref, lk2_ref, g_ref, q_ref, k_ref, v_ref, o_ref,
                      bias_ref, t_ref, *, seq, tq, rb_rows, kc, layer, lambda_init):
    h = pl.program_id(0)
    qi = pl.program_id(1)
    b = pl.program_id(2)
    n_kc = seq // kc

    @pl.when(b == 0)
    def _():
        tpos = qi * tq + lax.broadcasted_iota(I32, (tq, seq), 0)
        spos = lax.broadcasted_iota(I32, (tq, seq), 1)
        bias_ref[...] = sl_ref[h] * jnp.abs(tpos - spos).astype(F32)

    lam = (jnp.exp(jnp.sum(lq1_ref[layer:layer + 1, :] * lk1_ref[layer:layer + 1, :], axis=-1, keepdims=True))
           - jnp.exp(jnp.sum(lq2_ref[layer:layer + 1, :] * lk2_ref[layer:layer + 1, :], axis=-1, keepdims=True))
           + lambda_init)

    sections = [(rb, c) for rb in range(tq // rb_rows) for c in range(2)]
    key_tiles = [slice(j * kc, (j + 1) * kc) for j in range(n_kc)]
    ms = {}
    for rb, c in sections:
        rs = slice(rb * rb_rows, (rb + 1) * rb_rows)
        cs = slice(c * HEAD_DIM, (c + 1) * HEAD_DIM)
        q = q_ref[rs, cs]
        mx = jnp.full((rb_rows, LANES), -jnp.inf, F32)
        for ks in key_tiles:
            t = _dot_t(q, k_ref[ks, cs]) - bias_ref[rs, ks]
            t_ref[c, rs, ks] = t
            mx = functools.reduce(jnp.maximum, _lane_tiles(t), mx)
        ms[rb, c] = jnp.max(mx, axis=-1, keepdims=True)
    outs = {}
    for rb, c in sections:
        rs = slice(rb * rb_rows, (rb + 1) * rb_rows)
        ls = jnp.zeros((rb_rows, LANES), F32)
        acc = jnp.zeros((rb_rows, 2 * HEAD_DIM), F32)
        for ks in key_tiles:
            p = jnp.exp2(t_ref[c, rs, ks] - ms[rb, c])
            ls = functools.reduce(jnp.add, _lane_tiles(p), ls)
            acc = acc + _dot(p.astype(BF16), v_ref[ks, :])
        outs[rb, c] = acc / jnp.sum(ls, axis=-1, keepdims=True)
    for rb in range(tq // rb_rows):
        rs = slice(rb * rb_rows, (rb + 1) * rb_rows)
        o = outs[rb, 0] - lam * outs[rb, 1]
        rms = lax.rsqrt(jnp.mean(o * o, axis=-1, keepdims=True) + LN_EPS)
        o_ref[rs, :] = (o * rms * g_ref[layer:layer + 1, :] * (1.0 - lambda_init)).astype(o_ref.dtype)


def _diff_attn(z, slopes2, lq1, lk1, lq2, lk2, subln_g, n_seq, seq, layer, lambda_init):
    hw = 2 * HEAD_DIM
    tq = _tile(seq, 1024)
    rb_rows = _tile(tq, 128)
    kc = _tile(seq, 256)
    nq = seq // tq
    q_blk0 = DIFF_Q_COL0 // hw
    k_blk0 = q_blk0 + DIFF_QK // hw
    v_blk0 = k_blk0 + DIFF_QK // hw
    whole = lambda a: pl.BlockSpec(a.shape, lambda h, qi, b: (0, 0))
    return pl.pallas_call(
        functools.partial(_diff_attn_kernel, seq=seq, tq=tq, rb_rows=rb_rows, kc=kc, layer=layer,
                          lambda_init=lambda_init),
        grid=(DIFF_HEADS, nq, n_seq),
        in_specs=[
            pl.BlockSpec(memory_space=pltpu.SMEM),
            whole(lq1), whole(lk1), whole(lq2), whole(lk2), whole(subln_g),
            pl.BlockSpec((tq, hw), lambda h, qi, b: (b * nq + qi, q_blk0 + h)),
            pl.BlockSpec((seq, hw), lambda h, qi, b: (b, k_blk0 + h)),
            pl.BlockSpec((seq, hw), lambda h, qi, b: (b, v_blk0 + h)),
        ],
        out_specs=pl.BlockSpec((tq, hw), lambda h, qi, b: (b * nq + qi, h)),
        out_shape=jax.ShapeDtypeStruct((n_seq * seq, DIFF_V), BF16),
        scratch_shapes=[pltpu.VMEM((tq, seq), F32), pltpu.VMEM((2, tq, seq), F32)],
        compiler_params=_params(("arbitrary", "arbitrary", "arbitrary")),
        name="diff_attn",
    )(slopes2, lq1, lk1, lq2, lk2, subln_g, z, z, z)


def _residual_copy(x_hbm, buf, sem, tm):
    row0 = pl.multiple_of(pl.program_id(0) * tm, tm)
    return pltpu.make_async_copy(x_hbm.at[pl.ds(row0, tm), :], buf, sem)


def _route(hf, rhi_ref, rhl_ref):
    tm = hf.shape[0]
    h_hi = hf.astype(BF16)
    h_lo = (hf - h_hi.astype(F32)).astype(BF16)
    hh = _dot(h_hi, rhl_ref[...])
    logits = hh + pltpu.roll(hh, LANES - N_EXPERTS, 1) + _dot(h_lo, rhi_ref[...])
    lane = lax.broadcasted_iota(I32, logits.shape, 1)
    lg = jnp.where(lane < N_EXPERTS, logits, -jnp.inf)
    m1 = jnp.max(lg, axis=-1, keepdims=True)
    i1 = jnp.min(jnp.where(lg == m1, lane, LANES), axis=-1, keepdims=True)
    lg2 = jnp.where(lane == i1, -jnp.inf, lg)
    m2 = jnp.max(lg2, axis=-1, keepdims=True)
    i2 = jnp.min(jnp.where(lg2 == m2, lane, LANES), axis=-1, keepdims=True)
    e2 = jnp.exp(m2 - m1)
    g1 = 1.0 / (1.0 + e2)
    g2 = e2 / (1.0 + e2)
    oh1 = lane == i1
    oh2 = lane == i2
    sel = jnp.logical_or(oh1, oh2).astype(F32)
    r_idx = lax.broadcasted_iota(I32, (tm, tm), 0)
    c_idx = lax.broadcasted_iota(I32, (tm, tm), 1)
    lower = (c_idx < r_idx).astype(F32).astype(BF16)
    before = _dot(lower, sel.astype(BF16))
    rank1 = jnp.sum(jnp.where(oh1, before, 0.0), axis=-1, keepdims=True)
    rank2 = jnp.sum(jnp.where(oh2, before, 0.0), axis=-1, keepdims=True)
    counts = jnp.sum(sel, axis=0, keepdims=True)
    rec = jnp.zeros(logits.shape, F32)
    for ln, val in ((R_E1, i1.astype(F32)), (R_E2, i2.astype(F32)), (R_G1, g1), (R_G2, g2),
                    (R_RANK1, rank1), (R_RANK2, rank2)):
        rec = jnp.where(lane == ln, val, rec)
    return rec, counts


def _merge_kernel(*refs, alpha, with_router, d, tn):
    if with_router:
        (xb_ref, ya_ref, yd_ref, xf_ref, wg_ref, wpa_ref, wpd_ref, wo_ref, g_ref, b_ref,
         rhi_ref, rhl_ref, hf_ref, rec_ref, cnt_ref) = refs
    else:
        (xb_ref, ya_ref, yd_ref, xf_ref, wg_ref, wpa_ref, wpd_ref, wo_ref, g_ref, b_ref, hf_ref, hb_ref) = refs
    x = xb_ref[...]
    ya = ya_ref[...]
    yd = yd_ref[...]
    mix = None
    for j in range(d // tn):
        cs = slice(j * tn, (j + 1) * tn)
        ga = _dot(x, wg_ref[:, cs])
        gd = _dot(x, wg_ref[:, d + j * tn:d + (j + 1) * tn])
        pa = _dot(ya, wpa_ref[:, cs])
        pd = _dot(yd, wpd_ref[:, cs])
        merged = (jax.nn.sigmoid(ga) * pa + jax.nn.sigmoid(gd) * pd).astype(BF16)
        part = _dot(merged, wo_ref[cs, :])
        mix = part if mix is None else mix + part
    hf = _layer_norm(alpha * xf_ref[...] + mix, g_ref[...], b_ref[...])
    hf_ref[...] = hf
    if with_router:
        rec, counts = _route(hf, rhi_ref, rhl_ref)
        rec_ref[...] = rec
        cnt_ref[...] = jnp.broadcast_to(counts, cnt_ref.shape)
    else:
        hb_ref[...] = hf.astype(BF16)


def _merge(xb, ya, yd, xf, w_g, w_pa, w_pd, w_o, layer, ln_g, ln_b, alpha, router=None):
    t, d = xb.shape
    tm = _tile(t, MERGE_ROW_TILE)
    tn = _tile(d, 512)
    with_router = router is not None
    row = lambda w: pl.BlockSpec((tm, w), lambda i: (i, 0))

    def resident(a):
        return pl.BlockSpec((None,) + a.shape[1:], lambda i: (layer, 0, 0), pipeline_mode=pl.Buffered(1))

    in_specs = [
        row(d), row(WIN_Q), row(DIFF_V), row(d),
        resident(w_g), resident(w_pa), resident(w_pd), resident(w_o),
        pl.BlockSpec((1, d), lambda i: (0, 0)),
        pl.BlockSpec((1, d), lambda i: (0, 0)),
    ]
    args = [xb, ya, yd, xf, w_g, w_pa, w_pd, w_o, ln_g.reshape(1, d), ln_b.reshape(1, d)]
    out_specs = [row(d)]
    out_shape = [jax.ShapeDtypeStruct((t, d), F32)]
    if with_router:
        in_specs += [pl.BlockSpec((d, LANES), lambda i: (0, 0))] * 2
        args += list(router)
        out_specs += [row(LANES), pl.BlockSpec((8, LANES), lambda i: (i, 0))]
        out_shape += [jax.ShapeDtypeStruct((t, LANES), F32), jax.ShapeDtypeStruct((t // tm * 8, LANES), F32)]
    else:
        out_specs.append(row(d))
        out_shape.append(jax.ShapeDtypeStruct((t, d), BF16))
    return pl.pallas_call(
        functools.partial(_merge_kernel, alpha=alpha, with_router=with_router, d=d, tn=tn),
        grid=(t // tm,),
        in_specs=in_specs,
        out_specs=out_specs,
        out_shape=out_shape,
        compiler_params=_params(("arbitrary",)),
        name="merge_router" if with_router else "merge",
    )(*args)


def _swiglu(x, w1, w3):
    u = _dot(x, w1)
    return (u * jax.nn.sigmoid(u) * _dot(x, w3)).astype(BF16)


def _ffn_kernel(hb_ref, hf_hbm, w1_ref, w3_ref, w2_ref, g_ref, b_ref, of_ref, ob_ref, hf_buf, hf_sem, *, alpha, tm):
    f = pl.program_id(1)

    @pl.when(f == 0)
    def _():
        _residual_copy(hf_hbm, hf_buf, hf_sem, tm).start()
        of_ref[...] = jnp.zeros(of_ref.shape, F32)

    of_ref[...] += _dot(_swiglu(hb_ref[...], w1_ref[...], w3_ref[...]), w2_ref[...])

    @pl.when(f == pl.num_programs(1) - 1)
    def _():
        _residual_copy(hf_hbm, hf_buf, hf_sem, tm).wait()
        y = _layer_norm(alpha * hf_buf[...] + of_ref[...], g_ref[...], b_ref[...])
        of_ref[...] = y
        ob_ref[...] = y.astype(BF16)


def _ffn(hb, hf, w1, w3, w2, idx, ln_g, ln_b, alpha):
    t, d = hb.shape
    ff = w1.shape[2]
    tm = _tile(t, FFN_ROW_TILE)
    tf = _tile(ff, 512)
    row = pl.BlockSpec((tm, d), lambda i, f: (i, 0))
    return pl.pallas_call(
        functools.partial(_ffn_kernel, alpha=alpha, tm=tm),
        grid=(t // tm, ff // tf),
        in_specs=[
            row,
            pl.BlockSpec(memory_space=pl.ANY),
            pl.BlockSpec((None, d, tf), lambda i, f: (idx, 0, f)),
            pl.BlockSpec((None, d, tf), lambda i, f: (idx, 0, f)),
            pl.BlockSpec((None, tf, d), lambda i, f: (idx, f, 0)),
            pl.BlockSpec((1, d), lambda i, f: (0, 0)),
            pl.BlockSpec((1, d), lambda i, f: (0, 0)),
        ],
        out_specs=[row, row],
        out_shape=[jax.ShapeDtypeStruct((t, d), F32), jax.ShapeDtypeStruct((t, d), BF16)],
        scratch_shapes=[pltpu.VMEM((tm, d), F32), pltpu.SemaphoreType.DMA(())],
        compiler_params=_params(("arbitrary", "arbitrary")),
        name="ffn",
    )(hb, hf, w1, w3, w2, ln_g.reshape(1, d), ln_b.reshape(1, d))


def _moe_plan(rec, cnt, tm_route, tmx):
    t = rec.shape[0]
    e1 = rec[:, R_E1].astype(I32)
    e2 = rec[:, R_E2].astype(I32)
    rank1 = rec[:, R_RANK1].astype(I32)
    rank2 = rec[:, R_RANK2].astype(I32)
    tile_cnt = cnt.reshape(t // tm_route, 8, LANES)[:, 0, :N_EXPERTS].astype(I32)
    counts = jnp.sum(tile_cnt, axis=0)
    tile_base = jnp.cumsum(tile_cnt, axis=0) - tile_cnt
    pad_cnt = (counts + tmx - 1) // tmx * tmx
    pad_end = jnp.cumsum(pad_cnt)
    pad_start = pad_end - pad_cnt
    base = jnp.repeat(tile_base + pad_start[None, :], tm_route, axis=0)
    eids = jnp.arange(N_EXPERTS, dtype=I32)[None, :]
    pos1 = jnp.sum(jnp.where(e1[:, None] == eids, base, 0), axis=-1) + rank1
    pos2 = jnp.sum(jnp.where(e2[:, None] == eids, base, 0), axis=-1) + rank2
    n_rows = 2 * t + N_EXPERTS * tmx
    jj = jnp.arange(tmx, dtype=I32)[None, :]
    dummy_pos = jnp.where(jj < (pad_cnt - counts)[:, None], (pad_start + counts)[:, None] + jj, n_rows)
    tok = jnp.arange(t, dtype=I32)
    keys = jnp.concatenate([pos1, pos2, dummy_pos.reshape(-1)])
    vals = jnp.concatenate([tok, tok, jnp.zeros((N_EXPERTS * tmx,), I32)])
    _, src = lax.sort((keys, vals), num_keys=1)
    n_tiles = n_rows // tmx
    tile_start = jnp.arange(n_tiles, dtype=I32) * tmx
    tile_expert = jnp.minimum(jnp.sum((tile_start[:, None] >= pad_end[None, :]).astype(I32), axis=-1), N_EXPERTS - 1)
    tile_valid = (tile_start < pad_end[-1]).astype(I32)
    return pos1, pos2, src, tile_expert, tile_valid


def _row_gather(idx_ref, src_hbm, dst, sem, n):
    for r in range(n):
        pltpu.make_async_copy(src_hbm.at[pl.ds(idx_ref[0, 0, r], 1), :], dst.at[pl.ds(r, 1), :], sem).start()


def _rows_wait(src_hbm, dst, sem, n):
    pltpu.make_async_copy(src_hbm.at[pl.ds(0, n), :], dst, sem).wait()


def _experts_kernel(te_ref, tv_ref, cur_ref, nxt_ref, x_hbm, w1_ref, w3_ref, w2_ref, o_ref, xbuf, xb_ref, sems, *,
                    tmx):
    j = pl.program_id(0)
    n = pl.num_programs(0)
    slot = j % 2
    valid = tv_ref[j] > 0
    prev_valid = tv_ref[jnp.maximum(j - 1, 0)] > 0

    @pl.when(j == 0)
    def _():
        _row_gather(cur_ref, x_hbm, xbuf.at[0], sems.at[0], tmx)

    @pl.when(valid)
    def _():
        _row_gather(nxt_ref, x_hbm, xbuf.at[1 - slot], sems.at[1 - slot], tmx)
        _rows_wait(x_hbm, xbuf.at[slot], sems.at[slot], tmx)
        xb_ref[...] = xbuf[slot].astype(BF16)
        o_ref[...] = _dot(_swiglu(xb_ref[...], w1_ref[...], w3_ref[...]), w2_ref[...])

    @pl.when(jnp.logical_not(valid))
    def _():
        o_ref[...] = jnp.zeros(o_ref.shape, F32)

    @pl.when(jnp.logical_and(jnp.logical_not(valid), jnp.logical_and(prev_valid, j > 0)))
    def _():
        _rows_wait(x_hbm, xbuf.at[slot], sems.at[slot], tmx)

    @pl.when(jnp.logical_and(valid, j == n - 1))
    def _():
        _rows_wait(x_hbm, xbuf.at[1 - slot], sems.at[1 - slot], tmx)


def _experts(hf, src, tile_expert, tile_valid, w1, w3, w2, idx, tmx):
    t, d = hf.shape
    ff = w1.shape[3]
    n_tiles = src.shape[0] // tmx
    src3 = src.reshape(n_tiles, 1, tmx)
    grid_spec = pltpu.PrefetchScalarGridSpec(
        num_scalar_prefetch=2,
        grid=(n_tiles,),
        in_specs=[
            pl.BlockSpec((1, 1, tmx), lambda j, te, tv: (j, 0, 0), memory_space=pltpu.SMEM),
            pl.BlockSpec((1, 1, tmx), lambda j, te, tv: (jnp.minimum(j + 1, n_tiles - 1), 0, 0),
                         memory_space=pltpu.SMEM),
            pl.BlockSpec(memory_space=pl.ANY),
            pl.BlockSpec((None, None, d, ff), lambda j, te, tv: (idx, te[j], 0, 0)),
            pl.BlockSpec((None, None, d, ff), lambda j, te, tv: (idx, te[j], 0, 0)),
            pl.BlockSpec((None, None, ff, d), lambda j, te, tv: (idx, te[j], 0, 0)),
        ],
        out_specs=pl.BlockSpec((tmx, d), lambda j, te, tv: (j, 0)),
        scratch_shapes=[pltpu.VMEM((2, tmx, d), F32), pltpu.VMEM((tmx, d), BF16), pltpu.SemaphoreType.DMA((2,))],
    )
    return pl.pallas_call(
        functools.partial(_experts_kernel, tmx=tmx),
        grid_spec=grid_spec,
        out_shape=jax.ShapeDtypeStruct((n_tiles * tmx, d), F32),
        compiler_params=_params(("arbitrary",)),
        name="moe_experts",
    )(tile_expert, tile_valid, src3, src3, hf, w1, w3, w2)


def _combine_kernel(p1c_ref, p2c_ref, p1n_ref, p2n_ref, hf_ref, rec_ref, o_hbm, g_ref, b_ref, out0_ref, out1_ref,
                    obuf, sems, *, alpha, tmc, split_tile):
    j = pl.program_id(0)
    n = pl.num_programs(0)
    slot = j % 2

    def gather(p1_ref, p2_ref, s):
        _row_gather(p1_ref, o_hbm, obuf.at[s, 0], sems.at[s], tmc)
        _row_gather(p2_ref, o_hbm, obuf.at[s, 1], sems.at[s], tmc)

    def wait(s):
        _rows_wait(o_hbm, obuf.at[s, 0], sems.at[s], tmc)
        _rows_wait(o_hbm, obuf.at[s, 1], sems.at[s], tmc)

    @pl.when(j == 0)
    def _():
        gather(p1c_ref, p2c_ref, 0)

    gather(p1n_ref, p2n_ref, 1 - slot)
    wait(slot)
    rec = rec_ref[...]
    g1 = rec[:, R_G1:R_G1 + 1]
    g2 = rec[:, R_G2:R_G2 + 1]
    y = _layer_norm(alpha * hf_ref[...] + g1 * obuf[slot, 0] + g2 * obuf[slot, 1], g_ref[...], b_ref[...])
    if split_tile is None:
        out0_ref[...] = y
        out1_ref[...] = y.astype(BF16)
    else:
        @pl.when(j < split_tile)
        def _():
            out0_ref[...] = y

        @pl.when(j >= split_tile)
        def _():
            out1_ref[...] = y

    @pl.when(j == n - 1)
    def _():
        wait(1 - slot)


def _combine(hf, rec, o_sorted, pos1, pos2, ln_g, ln_b, alpha, split_rows=None):
    t, d = hf.shape
    tmc = _tile(t if split_rows is None else math.gcd(t, split_rows), COMBINE_TILE)
    n_tiles = t // tmc
    p1 = pos1.reshape(n_tiles, 1, tmc)
    p2 = pos2.reshape(n_tiles, 1, tmc)
    cur = pl.BlockSpec((1, 1, tmc), lambda j: (j, 0, 0), memory_space=pltpu.SMEM)
    nxt = pl.BlockSpec((1, 1, tmc), lambda j: (jnp.minimum(j + 1, n_tiles - 1), 0, 0), memory_space=pltpu.SMEM)
    row = lambda w: pl.BlockSpec((tmc, w), lambda j: (j, 0))
    if split_rows is None:
        split_tile = None
        out_specs = [row(d), row(d)]
        out_shape = [jax.ShapeDtypeStruct((t, d), F32), jax.ShapeDtypeStruct((t, d), BF16)]
    else:
        split_tile = split_rows // tmc
        out_specs = [pl.BlockSpec((tmc, d), lambda j: (jnp.minimum(j, split_tile - 1), 0)),
                     pl.BlockSpec((tmc, d), lambda j: (jnp.maximum(j - split_tile, 0), 0))]
        out_shape = [jax.ShapeDtypeStruct((split_rows, d), F32), jax.ShapeDtypeStruct((t - split_rows, d), F32)]
    return pl.pallas_call(
        functools.partial(_combine_kernel, alpha=alpha, tmc=tmc, split_tile=split_tile),
        grid=(n_tiles,),
        in_specs=[cur, cur, nxt, nxt, row(d), row(LANES), pl.BlockSpec(memory_space=pl.ANY),
                  pl.BlockSpec((1, d), lambda j: (0, 0)), pl.BlockSpec((1, d), lambda j: (0, 0))],
        out_specs=out_specs,
        out_shape=out_shape,
        scratch_shapes=[pltpu.VMEM((2, 2, tmc, d), F32), pltpu.SemaphoreType.DMA((2,))],
        compiler_params=_params(("arbitrary",)),
        name="moe_combine",
    )(p1, p2, p1, p2, hf, rec, o_sorted, ln_g.reshape(1, d), ln_b.reshape(1, d))


def _pad_to(a, axis, mult):
    pad = (-a.shape[axis]) % mult
    if pad == 0:
        return a
    widths = [(0, 0)] * a.ndim
    widths[axis] = (0, pad)
    return jnp.pad(a, widths)


def kernel(x_prompt, x_sample, ln_emb_g, ln_emb_b, w_in, sink, lambda_q1, lambda_k1, lambda_q2, lambda_k2,
           subln_g, w_pa, w_pd, w_o, ln1_g, ln1_b, ln2_g, ln2_b, ffn_w1, ffn_w3, ffn_w2, router_w,
           moe_w1, moe_w3, moe_w2):
    bp, seq, d = x_prompt.shape
    bs, seq_s, _ = x_sample.shape
    assert seq == seq_s
    depth = w_in.shape[0]
    n_seq = bp + bs
    t = n_seq * seq
    alpha = (2.0 * depth) ** 0.25
    slopes = [2.0 ** (-8.0 * (h + 1) / N_ALIBI_HEADS) for h in range(N_ALIBI_HEADS)]
    win_slopes2 = jnp.asarray([s * LOG2E for s in slopes[:WIN_Q_HEADS]], F32)
    diff_slopes2 = jnp.asarray([s * LOG2E for s in slopes[WIN_Q_HEADS:]], F32)
    col = jnp.arange(QKV_COLS)
    is_q = (col < WIN_Q) | ((col >= DIFF_Q_COL0) & (col < DIFF_Q_COL0 + DIFF_QK))
    col_scale = jnp.where(is_q, Q_SCALE, 1.0).astype(F32).reshape(1, QKV_COLS)
    tm_route = _tile(t, MERGE_ROW_TILE)

    w_qkv_b = w_in[:, :, :QKV_COLS].astype(BF16)
    w_g_b = w_in[:, :, QKV_COLS:].astype(BF16)
    w_pa_b, w_pd_b, w_o_b = (w.astype(BF16) for w in (w_pa, w_pd, w_o))
    ffn_w1_b = _pad_to(ffn_w1.astype(BF16), 2, 512)
    ffn_w3_b = _pad_to(ffn_w3.astype(BF16), 2, 512)
    ffn_w2_b = _pad_to(ffn_w2.astype(BF16), 1, 512)
    moe_w1_b, moe_w3_b, moe_w2_b = (w.astype(BF16) for w in (moe_w1, moe_w3, moe_w2))

    xf, xb = _embed_ln(x_prompt.reshape(bp * seq, d), x_sample.reshape(bs * seq, d), ln_emb_g, ln_emb_b)

    outs = None
    for l in range(depth):
        lambda_init = 0.8 - 0.6 * math.exp(-0.3 * l)
        z = _inproj(xb, w_qkv_b, l, col_scale)
        ya = _win_attn(z, jnp.concatenate([sink[l].astype(F32) * LOG2E, win_slopes2]), n_seq, seq)
        yd = _diff_attn(z, diff_slopes2, lambda_q1, lambda_k1, lambda_q2, lambda_k2, subln_g,
                        n_seq, seq, l, lambda_init)
        merge_args = (xb, ya, yd, xf, w_g_b, w_pa_b, w_pd_b, w_o_b, l, ln1_g[l], ln1_b[l], alpha)
        i = l // 2
        last = l == depth - 1
        if l % 2 == 0:
            hf, hb = _merge(*merge_args)
            xf, xb = _ffn(hb, hf, ffn_w1_b, ffn_w3_b, ffn_w2_b, i, ln2_g[l], ln2_b[l], alpha)
        else:
            rw = _pad_to(router_w[i].astype(F32), 1, LANES)
            r_hi = rw.astype(BF16)
            r_lo = (rw - r_hi.astype(F32)).astype(BF16)
            r_hl = r_hi + jnp.roll(r_lo, N_EXPERTS, axis=1)
            hf, rec, cnt = _merge(*merge_args, router=(r_hi, r_hl))
            pos1, pos2, src, tile_expert, tile_valid = _moe_plan(rec, cnt, tm_route, ROW_TILE)
            o_sorted = _experts(hf, src, tile_expert, tile_valid, moe_w1_b, moe_w3_b, moe_w2_b, i, ROW_TILE)
            res = _combine(hf, rec, o_sorted, pos1, pos2, ln2_g[l], ln2_b[l], alpha,
                           split_rows=bp * seq if last else None)
            if last:
                outs = res
            else:
                xf, xb = res

    if outs is None:
        outs = (xf[:bp * seq], xf[bp * seq:])
    return (outs[0].reshape(bp, seq, d), outs[1].reshape(bs, seq, d))
```

```python
import functools
import math

import jax
import jax.numpy as jnp
from jax import lax
from jax.experimental import pallas as pl
from jax.experimental.pallas import tpu as pltpu

HEAD_DIM = 128
WIN_Q_HEADS = 8
WIN_KV_HEADS = 2
WIN_REP = WIN_Q_HEADS // WIN_KV_HEADS
WINDOW = 128
BLOCK = 128
DIFF_HEADS = 4
N_ALIBI_HEADS = WIN_Q_HEADS + DIFF_HEADS
WIN_Q = WIN_Q_HEADS * HEAD_DIM
WIN_KV = WIN_KV_HEADS * HEAD_DIM
DIFF_QK = DIFF_HEADS * 2 * HEAD_DIM
DIFF_V = DIFF_HEADS * 2 * HEAD_DIM
QKV_COLS = WIN_Q + 2 * WIN_KV + 2 * DIFF_QK + DIFF_V
DIFF_Q_COL0 = WIN_Q + 2 * WIN_KV
N_EXPERTS = 8
LN_EPS = 1e-5
NEG_INF = -1e30
LOG2E = 1.4426950408889634
Q_SCALE = (HEAD_DIM ** -0.5) * LOG2E
LANES = 128
R_E1, R_E2, R_G1, R_G2, R_RANK1, R_RANK2 = range(6)

VMEM_LIMIT = 56 * 1024 * 1024
F32 = jnp.float32
BF16 = jnp.bfloat16
I32 = jnp.int32

ROW_TILE = 512
MERGE_ROW_TILE = 256
FFN_ROW_TILE = 768
COMBINE_TILE = 256


def _params(sem, vmem=VMEM_LIMIT):
    return pltpu.CompilerParams(dimension_semantics=sem, vmem_limit_bytes=vmem)


def _tile(n, want):
    for t in range(min(n, want), 0, -1):
        if n % t == 0 and (t % 8 == 0 or t == n):
            return t
    raise ValueError((n, want))


def _layer_norm(v, g, b):
    mu = jnp.mean(v, axis=-1, keepdims=True)
    xc = v - mu
    var = jnp.mean(xc * xc, axis=-1, keepdims=True)
    return xc * lax.rsqrt(var + LN_EPS) * g + b


def _dot(a, b):
    return jnp.dot(a, b, preferred_element_type=F32)


def _dot_t(a, b):
    return lax.dot_general(a, b, (((1,), (1,)), ((), ())), preferred_element_type=F32)


def _embed_ln_kernel(xp_ref, xs_ref, g_ref, b_ref, of_ref, ob_ref, *, n_prompt_tiles):
    i = pl.program_id(0)

    def emit(x_ref):
        y = _layer_norm(x_ref[...], g_ref[...], b_ref[...])
        of_ref[...] = y
        ob_ref[...] = y.astype(BF16)

    @pl.when(i < n_prompt_tiles)
    def _():
        emit(xp_ref)

    @pl.when(i >= n_prompt_tiles)
    def _():
        emit(xs_ref)


def _embed_ln(xp, xs, g, b):
    tp, d = xp.shape
    ts = xs.shape[0]
    tm = _tile(math.gcd(tp, ts), 512)
    npt, nst = tp // tm, ts // tm
    t = tp + ts
    return pl.pallas_call(
        functools.partial(_embed_ln_kernel, n_prompt_tiles=npt),
        grid=(npt + nst,),
        in_specs=[
            pl.BlockSpec((tm, d), lambda i: (jnp.minimum(i, npt - 1), 0)),
            pl.BlockSpec((tm, d), lambda i: (jnp.maximum(i - npt, 0), 0)),
            pl.BlockSpec((1, d), lambda i: (0, 0)),
            pl.BlockSpec((1, d), lambda i: (0, 0)),
        ],
        out_specs=[
            pl.BlockSpec((tm, d), lambda i: (i, 0)),
            pl.BlockSpec((tm, d), lambda i: (i, 0)),
        ],
        out_shape=[jax.ShapeDtypeStruct((t, d), F32), jax.ShapeDtypeStruct((t, d), BF16)],
        compiler_params=_params(("arbitrary",)),
        name="embed_ln",
    )(xp, xs, g.reshape(1, d), b.reshape(1, d))


def _inproj_kernel(x_ref, w_ref, s_ref, o_ref):
    o_ref[...] = (_dot(x_ref[...], w_ref[...]) * s_ref[...]).astype(o_ref.dtype)


def _inproj(xb, w_in, layer, col_scale):
    t, d = xb.shape
    n = QKV_COLS
    tm = _tile(t, 1024)
    tn = _tile(n, 1536)
    return pl.pallas_call(
        _inproj_kernel,
        grid=(t // tm, n // tn),
        in_specs=[
            pl.BlockSpec((tm, d), lambda i, j: (i, 0)),
            pl.BlockSpec((None, d, tn), lambda i, j: (layer, 0, j)),
            pl.BlockSpec((1, tn), lambda i, j: (0, j)),
        ],
        out_specs=pl.BlockSpec((tm, tn), lambda i, j: (i, j)),
        out_shape=jax.ShapeDtypeStruct((t, n), BF16),
        compiler_params=_params(("parallel", "arbitrary")),
        name="inproj",
    )(xb, w_in, col_scale)


def _lane_tiles(x):
    return [x[:, u * LANES:(u + 1) * LANES] for u in range(x.shape[1] // LANES)]


def _win_attn_kernel(sc_ref, q_ref, k_ref, v_ref, o_ref, t_ref, *, seq):
    g = pl.program_id(1)
    kw = min(3 * BLOCK, seq)
    nb = seq // BLOCK
    row = lax.broadcasted_iota(I32, (BLOCK, kw), 0)
    col = lax.broadcasted_iota(I32, (BLOCK, kw), 1)
    k_starts = [min(max(n * BLOCK - BLOCK, 0), seq - kw) for n in range(nb)]
    offsets = sorted({n * BLOCK - k_starts[n] for n in range(nb)})
    dists = {off: jnp.abs(row - col + off) for off in offsets}
    head_cols = [slice(r * HEAD_DIM, (r + 1) * HEAD_DIM) for r in range(WIN_REP)]
    ms = {}
    for r, cs in enumerate(head_cols):
        sink2 = sc_ref[g * WIN_REP + r]
        slope2 = sc_ref[WIN_Q_HEADS + g * WIN_REP + r]
        bias = {off: jnp.where(d <= WINDOW, slope2 * d.astype(F32), -NEG_INF) for off, d in dists.items()}
        for n in range(nb):
            qs = slice(n * BLOCK, (n + 1) * BLOCK)
            ks = slice(k_starts[n], k_starts[n] + kw)
            t = _dot_t(q_ref[qs, cs], k_ref[ks, :]) - bias[n * BLOCK - k_starts[n]]
            t_ref[r, qs, :] = t
            mx = functools.reduce(jnp.maximum, _lane_tiles(t))
            ms[r, n] = jnp.maximum(jnp.max(mx, axis=-1, keepdims=True), sink2)
    for r, cs in enumerate(head_cols):
        sink2 = sc_ref[g * WIN_REP + r]
        for n in range(nb):
            qs = slice(n * BLOCK, (n + 1) * BLOCK)
            ks = slice(k_starts[n], k_starts[n] + kw)
            m = ms[r, n]
            p = jnp.exp2(t_ref[r, qs, :] - m)
            ls = functools.reduce(jnp.add, _lane_tiles(p))
            denom = jnp.sum(ls, axis=-1, keepdims=True) + jnp.exp2(sink2 - m)
            o_ref[qs, cs] = (_dot(p.astype(BF16), v_ref[ks, :]) / denom).astype(o_ref.dtype)


def _win_attn(z, scalars, n_seq, seq):
    gq = WIN_REP * HEAD_DIM
    kw = min(3 * BLOCK, seq)
    k_blk0 = WIN_Q // HEAD_DIM
    v_blk0 = (WIN_Q + WIN_KV) // HEAD_DIM
    return pl.pallas_call(
        functools.partial(_win_attn_kernel, seq=seq),
        grid=(n_seq, WIN_KV_HEADS),
        in_specs=[
            pl.BlockSpec(memory_space=pltpu.SMEM),
            pl.BlockSpec((seq, gq), lambda b, g: (b, g)),
            pl.BlockSpec((seq, HEAD_DIM), lambda b, g: (b, k_blk0 + g)),
            pl.BlockSpec((seq, HEAD_DIM), lambda b, g: (b, v_blk0 + g)),
        ],
        out_specs=pl.BlockSpec((seq, gq), lambda b, g: (b, g)),
        out_shape=jax.ShapeDtypeStruct((n_seq * seq, WIN_Q), BF16),
        scratch_shapes=[pltpu.VMEM((WIN_REP, seq, kw), F32)],
        compiler_params=_params(("parallel", "arbitrary")),
        name="win_attn",
    )(scalars, z, z, z)


def _diff_attn_kernel(sl_ref, lq1_ref, lk1_ref, lq2_ref, lk2_ref, g_ref, q_ref, k_ref, v_ref, o_ref,
                      bias_ref, t_ref, *, seq, tq, rb_rows, kc, layer, lambda_init):
    h = pl.program_id(0)
    qi = pl.program_id(1)
    b = pl.program_id(2)
    n_kc = seq // kc

    @pl.when(b == 0)
    def _():
        tpos = qi * tq + lax.broadcasted_iota(I32, (tq, seq), 0)
        spos = lax.broadcasted_iota(I32, (tq, seq), 1)
        bias_ref[...] = sl_ref[h] * jnp.abs(tpos - spos).astype(F32)

    lam = (jnp.exp(jnp.sum(lq1_ref[layer:layer + 1, :] * lk1_ref[layer:layer + 1, :], axis=-1, keepdims=True))
           - jnp.exp(jnp.sum(lq2_ref[layer:layer + 1, :] * lk2_ref[layer:layer + 1, :], axis=-1, keepdims=True))
           + lambda_init)

    sections = [(rb, c) for rb in range(tq // rb_rows) for c in range(2)]
    key_tiles = [slice(j * kc, (j + 1) * kc) for j in range(n_kc)]
    ms = {}
    for rb, c in sections:
        rs = slice(rb * rb_rows, (rb + 1) * rb_rows)
        cs = slice(c * HEAD_DIM, (c + 1) * HEAD_DIM)
        q = q_ref[rs, cs]
        mx = jnp.full((rb_rows, LANES), -jnp.inf, F32)
        for ks in key_tiles:
            t = _dot_t(q, k_ref[ks, cs]) - bias_ref[rs, ks]
            t_ref[c, rs, ks] = t
            mx = functools.reduce(jnp.maximum, _lane_tiles(t), mx)
        ms[rb, c] = jnp.max(mx, axis=-1, keepdims=True)
    outs = {}
    for rb, c in sections:
        rs = slice(rb * rb_rows, (rb + 1) * rb_rows)
        ls = jnp.zeros((rb_rows, LANES), F32)
        acc = jnp.zeros((rb_rows, 2 * HEAD_DIM), F32)
        for ks in key_tiles:
            p = jnp.exp2(t_ref[c, rs, ks] - ms[rb, c])
            ls = functools.reduce(jnp.add, _lane_tiles(p), ls)
            acc = acc + _dot(p.astype(BF16), v_ref[ks, :])
        outs[rb, c] = acc / jnp.sum(ls, axis=-1, keepdims=True)
    for rb in range(tq // rb_rows):
        rs = slice(rb * rb_rows, (rb + 1) * rb_rows)
        o = outs[rb, 0] - lam * outs[rb, 1]
        rms = lax.rsqrt(jnp.mean(o * o, axis=-1, keepdims=True) + LN_EPS)
        o_ref[rs, :] = (o * rms * g_ref[layer:layer + 1, :] * (1.0 - lambda_init)).astype(o_ref.dtype)


def _diff_attn(z, slopes2, lq1, lk1, lq2, lk2, subln_g, n_seq, seq, layer, lambda_init):
    hw = 2 * HEAD_DIM
    tq = _tile(seq, 1024)
    rb_rows = _tile(tq, 128)
    kc = _tile(seq, 256)
    nq = seq // tq
    q_blk0 = DIFF_Q_COL0 // hw
    k_blk0 = q_blk0 + DIFF_QK // hw
    v_blk0 = k_blk0 + DIFF_QK // hw
    whole = lambda a: pl.BlockSpec(a.shape, lambda h, qi, b: (0, 0))
    return pl.pallas_call(
        functools.partial(_diff_attn_kernel, seq=seq, tq=tq, rb_rows=rb_rows, kc=kc, layer=layer,
                          lambda_init=lambda_init),
        grid=(DIFF_HEADS, nq, n_seq),
        in_specs=[
            pl.BlockSpec(memory_space=pltpu.SMEM),
            whole(lq1), whole(lk1), whole(lq2), whole(lk2), whole(subln_g),
            pl.BlockSpec((tq, hw), lambda h, qi, b: (b * nq + qi, q_blk0 + h)),
            pl.BlockSpec((seq, hw), lambda h, qi, b: (b, k_blk0 + h)),
            pl.BlockSpec((seq, hw), lambda h, qi, b: (b, v_blk0 + h)),
        ],
        out_specs=pl.BlockSpec((tq, hw), lambda h, qi, b: (b * nq + qi, h)),
        out_shape=jax.ShapeDtypeStruct((n_seq * seq, DIFF_V), BF16),
        scratch_shapes=[pltpu.VMEM((tq, seq), F32), pltpu.VMEM((2, tq, seq), F32)],
        compiler_params=_params(("arbitrary", "arbitrary", "arbitrary")),
        name="diff_attn",
    )(slopes2, lq1, lk1, lq2, lk2, subln_g, z, z, z)


def _residual_copy(x_hbm, buf, sem, tm):
    row0 = pl.multiple_of(pl.program_id(0) * tm, tm)
    return pltpu.make_async_copy(x_hbm.at[pl.ds(row0, tm), :], buf, sem)


def _route(hf, rhi_ref, rhl_ref):
    tm = hf.shape[0]
    h_hi = hf.astype(BF16)
    h_lo = (hf - h_hi.astype(F32)).astype(BF16)
    hh = _dot(h_hi, rhl_ref[...])
    logits = hh + pltpu.roll(hh, LANES - N_EXPERTS, 1) + _dot(h_lo, rhi_ref[...])
    lane = lax.broadcasted_iota(I32, logits.shape, 1)
    lg = jnp.where(lane < N_EXPERTS, logits, -jnp.inf)
    m1 = jnp.max(lg, axis=-1, keepdims=True)
    i1 = jnp.min(jnp.where(lg == m1, lane, LANES), axis=-1, keepdims=True)
    lg2 = jnp.where(lane == i1, -jnp.inf, lg)
    m2 = jnp.max(lg2, axis=-1, keepdims=True)
    i2 = jnp.min(jnp.where(lg2 == m2, lane, LANES), axis=-1, keepdims=True)
    e2 = jnp.exp(m2 - m1)
    g1 = 1.0 / (1.0 + e2)
    g2 = e2 / (1.0 + e2)
    oh1 = lane == i1
    oh2 = lane == i2
    sel = jnp.logical_or(oh1, oh2).astype(F32)
    r_idx = lax.broadcasted_iota(I32, (tm, tm), 0)
    c_idx = lax.broadcasted_iota(I32, (tm, tm), 1)
    lower = (c_idx < r_idx).astype(F32).astype(BF16)
    before = _dot(lower, sel.astype(BF16))
    rank1 = jnp.sum(jnp.where(oh1, before, 0.0), axis=-1, keepdims=True)
    rank2 = jnp.sum(jnp.where(oh2, before, 0.0), axis=-1, keepdims=True)
    counts = jnp.sum(sel, axis=0, keepdims=True)
    rec = jnp.zeros(logits.shape, F32)
    for ln, val in ((R_E1, i1.astype(F32)), (R_E2, i2.astype(F32)), (R_G1, g1), (R_G2, g2),
                    (R_RANK1, rank1), (R_RANK2, rank2)):
        rec = jnp.where(lane == ln, val, rec)
    return rec, counts


def _merge_kernel(*refs, alpha, with_router, d, tn):
    if with_router:
        (xb_ref, ya_ref, yd_ref, xf_ref, wg_ref, wpa_ref, wpd_ref, wo_ref, g_ref, b_ref,
         rhi_ref, rhl_ref, hf_ref, rec_ref, cnt_ref) = refs
    else:
        (xb_ref, ya_ref, yd_ref, xf_ref, wg_ref, wpa_ref, wpd_ref, wo_ref, g_ref, b_ref, hf_ref, hb_ref) = refs
    x = xb_ref[...]
    ya = ya_ref[...]
    yd = yd_ref[...]
    mix = None
    for j in range(d // tn):
        cs = slice(j * tn, (j + 1) * tn)
        ga = _dot(x, wg_ref[:, cs])
        gd = _dot(x, wg_ref[:, d + j * tn:d + (j + 1) * tn])
        pa = _dot(ya, wpa_ref[:, cs])
        pd = _dot(yd, wpd_ref[:, cs])
        merged = (jax.nn.sigmoid(ga) * pa + jax.nn.sigmoid(gd) * pd).astype(BF16)
        part = _dot(merged, wo_ref[cs, :])
        mix = part if mix is None else mix + part
    hf = _layer_norm(alpha * xf_ref[...] + mix, g_ref[...], b_ref[...])
    hf_ref[...] = hf
    if with_router:
        rec, counts = _route(hf, rhi_ref, rhl_ref)
        rec_ref[...] = rec
        cnt_ref[...] = jnp.broadcast_to(counts, cnt_ref.shape)
    else:
        hb_ref[...] = hf.astype(BF16)


def _merge(xb, ya, yd, xf, w_g, w_pa, w_pd, w_o, layer, ln_g, ln_b, alpha, router=None):
    t, d = xb.shape
    tm = _tile(t, MERGE_ROW_TILE)
    tn = _tile(d, 512)
    with_router = router is not None
    row = lambda w: pl.BlockSpec((tm, w), lambda i: (i, 0))

    def resident(a):
        return pl.BlockSpec((None,) + a.shape[1:], lambda i: (layer, 0, 0), pipeline_mode=pl.Buffered(1))

    in_specs = [
        row(d), row(WIN_Q), row(DIFF_V), row(d),
        resident(w_g), resident(w_pa), resident(w_pd), resident(w_o),
        pl.BlockSpec((1, d), lambda i: (0, 0)),
        pl.BlockSpec((1, d), lambda i: (0, 0)),
    ]
    args = [xb, ya, yd, xf, w_g, w_pa, w_pd, w_o, ln_g.reshape(1, d), ln_b.reshape(1, d)]
    out_specs = [row(d)]
    out_shape = [jax.ShapeDtypeStruct((t, d), F32)]
    if with_router:
        in_specs += [pl.BlockSpec((d, LANES), lambda i: (0, 0))] * 2
        args += list(router)
        out_specs += [row(LANES), pl.BlockSpec((8, LANES), lambda i: (i, 0))]
        out_shape += [jax.ShapeDtypeStruct((t, LANES), F32), jax.ShapeDtypeStruct((t // tm * 8, LANES), F32)]
    else:
        out_specs.append(row(d))
        out_shape.append(jax.ShapeDtypeStruct((t, d), BF16))
    return pl.pallas_call(
        functools.partial(_merge_kernel, alpha=alpha, with_router=with_router, d=d, tn=tn),
        grid=(t // tm,),
        in_specs=in_specs,
        out_specs=out_specs,
        out_shape=out_shape,
        compiler_params=_params(("arbitrary",)),
        name="merge_router" if with_router else "merge",
    )(*args)


def _swiglu(x, w1, w3):
    u = _dot(x, w1)
    return (u * jax.nn.sigmoid(u) * _dot(x, w3)).astype(BF16)


def _ffn_kernel(hb_ref, hf_hbm, w1_ref, w3_ref, w2_ref, g_ref, b_ref, of_ref, ob_ref, hf_buf, hf_sem, *, alpha, tm):
    f = pl.program_id(1)

    @pl.when(f == 0)
    def _():
        _residual_copy(hf_hbm, hf_buf, hf_sem, tm).start()
        of_ref[...] = jnp.zeros(of_ref.shape, F32)

    of_ref[...] += _dot(_swiglu(hb_ref[...], w1_ref[...], w3_ref[...]), w2_ref[...])

    @pl.when(f == pl.num_programs(1) - 1)
    def _():
        _residual_copy(hf_hbm, hf_buf, hf_sem, tm).wait()
        y = _layer_norm(alpha * hf_buf[...] + of_ref[...], g_ref[...], b_ref[...])
        of_ref[...] = y
        ob_ref[...] = y.astype(BF16)


def _ffn(hb, hf, w1, w3, w2, idx, ln_g, ln_b, alpha):
    t, d = hb.shape
    ff = w1.shape[2]
    tm = _tile(t, FFN_ROW_TILE)
    tf = _tile(ff, 512)
    row = pl.BlockSpec((tm, d), lambda i, f: (i, 0))
    return pl.pallas_call(
        functools.partial(_ffn_kernel, alpha=alpha, tm=tm),
        grid=(t // tm, ff // tf),
        in_specs=[
            row,
            pl.BlockSpec(memory_space=pl.ANY),
            pl.BlockSpec((None, d, tf), lambda i, f: (idx, 0, f)),
            pl.BlockSpec((None, d, tf), lambda i, f: (idx, 0, f)),
            pl.BlockSpec((None, tf, d), lambda i, f: (idx, f, 0)),
            pl.BlockSpec((1, d), lambda i, f: (0, 0)),
            pl.BlockSpec((1, d), lambda i, f: (0, 0)),
        ],
        out_specs=[row, row],
        out_shape=[jax.ShapeDtypeStruct((t, d), F32), jax.ShapeDtypeStruct((t, d), BF16)],
        scratch_shapes=[pltpu.VMEM((tm, d), F32), pltpu.SemaphoreType.DMA(())],
        compiler_params=_params(("arbitrary", "arbitrary")),
        name="ffn",
    )(hb, hf, w1, w3, w2, ln_g.reshape(1, d), ln_b.reshape(1, d))


def _moe_plan(rec, cnt, tm_route, tmx):
    t = rec.shape[0]
    e1 = rec[:, R_E1].astype(I32)
    e2 = rec[:, R_E2].astype(I32)
    rank1 = rec[:, R_RANK1].astype(I32)
    rank2 = rec[:, R_RANK2].astype(I32)
    tile_cnt = cnt.reshape(t // tm_route, 8, LANES)[:, 0, :N_EXPERTS].astype(I32)
    counts = jnp.sum(tile_cnt, axis=0)
    tile_base = jnp.cumsum(tile_cnt, axis=0) - tile_cnt
    pad_cnt = (counts + tmx - 1) // tmx * tmx
    pad_end = jnp.cumsum(pad_cnt)
    pad_start = pad_end - pad_cnt
    base = jnp.repeat(tile_base + pad_start[None, :], tm_route, axis=0)
    eids = jnp.arange(N_EXPERTS, dtype=I32)[None, :]
    pos1 = jnp.sum(jnp.where(e1[:, None] == eids, base, 0), axis=-1) + rank1
    pos2 = jnp.sum(jnp.where(e2[:, None] == eids, base, 0), axis=-1) + rank2
    n_rows = 2 * t + N_EXPERTS * tmx
    jj = jnp.arange(tmx, dtype=I32)[None, :]
    dummy_pos = jnp.where(jj < (pad_cnt - counts)[:, None], (pad_start + counts)[:, None] + jj, n_rows)
    tok = jnp.arange(t, dtype=I32)
    keys = jnp.concatenate([pos1, pos2, dummy_pos.reshape(-1)])
    vals = jnp.concatenate([tok, tok, jnp.zeros((N_EXPERTS * tmx,), I32)])
    _, src = lax.sort((keys, vals), num_keys=1)
    n_tiles = n_rows // tmx
    tile_start = jnp.arange(n_tiles, dtype=I32) * tmx
    tile_expert = jnp.minimum(jnp.sum((tile_start[:, None] >= pad_end[None, :]).astype(I32), axis=-1), N_EXPERTS - 1)
    tile_valid = (tile_start < pad_end[-1]).astype(I32)
    return pos1, pos2, src, tile_expert, tile_valid


def _row_gather(idx_ref, src_hbm, dst, sem, n):
    for r in range(n):
        pltpu.make_async_copy(src_hbm.at[pl.ds(idx_ref[0, 0, r], 1), :], dst.at[pl.ds(r, 1), :], sem).start()


def _rows_wait(src_hbm, dst, sem, n):
    pltpu.make_async_copy(src_hbm.at[pl.ds(0, n), :], dst, sem).wait()


GATHER_SLOTS = 3


def _experts_kernel(te_ref, tv_ref, ia_ref, ib_ref, ic_ref, x_hbm, w1_ref, w3_ref, w2_ref, o_ref, xbuf, xb_ref, sems,
                    *, tmx):
    j = pl.program_id(0)
    n = pl.num_programs(0)
    slot = j % GATHER_SLOTS
    valid = tv_ref[j] > 0

    def advance():
        nxt2 = (j + 2) % GATHER_SLOTS
        nxt1 = (j + 1) % GATHER_SLOTS
        _row_gather(ic_ref, x_hbm, xbuf.at[nxt2], sems.at[nxt2], tmx)
        _rows_wait(x_hbm, xbuf.at[nxt1], sems.at[nxt1], tmx)

    @pl.when(j == 0)
    def _():
        _row_gather(ia_ref, x_hbm, xbuf.at[0], sems.at[0], tmx)
        _row_gather(ib_ref, x_hbm, xbuf.at[1], sems.at[1], tmx)
        _rows_wait(x_hbm, xbuf.at[0], sems.at[0], tmx)

    @pl.when(valid)
    def _():
        xb_ref[...] = xbuf[slot].astype(BF16)
        o_ref[...] = _dot(_swiglu(xb_ref[...], w1_ref[...], w3_ref[...]), w2_ref[...])
        advance()

    @pl.when(jnp.logical_not(valid))
    def _():
        o_ref[...] = jnp.zeros(o_ref.shape, F32)
        advance()

    @pl.when(j == n - 1)
    def _():
        last = (j + 2) % GATHER_SLOTS
        _rows_wait(x_hbm, xbuf.at[last], sems.at[last], tmx)


def _experts(hf, src, tile_expert, tile_valid, w1, w3, w2, idx, tmx):
    t, d = hf.shape
    ff = w1.shape[3]
    n_tiles = src.shape[0] // tmx
    src3 = src.reshape(n_tiles, 1, tmx)

    def ahead(k):
        return pl.BlockSpec((1, 1, tmx), lambda j, te, tv: (jnp.minimum(j + k, n_tiles - 1), 0, 0),
                            memory_space=pltpu.SMEM)

    grid_spec = pltpu.PrefetchScalarGridSpec(
        num_scalar_prefetch=2,
        grid=(n_tiles,),
        in_specs=[
            ahead(0), ahead(1), ahead(2),
            pl.BlockSpec(memory_space=pl.ANY),
            pl.BlockSpec((None, None, d, ff), lambda j, te, tv: (idx, te[j], 0, 0)),
            pl.BlockSpec((None, None, d, ff), lambda j, te, tv: (idx, te[j], 0, 0)),
            pl.BlockSpec((None, None, ff, d), lambda j, te, tv: (idx, te[j], 0, 0)),
        ],
        out_specs=pl.BlockSpec((tmx, d), lambda j, te, tv: (j, 0)),
        scratch_shapes=[pltpu.VMEM((GATHER_SLOTS, tmx, d), F32), pltpu.VMEM((tmx, d), BF16),
                        pltpu.SemaphoreType.DMA((GATHER_SLOTS,))],
    )
    return pl.pallas_call(
        functools.partial(_experts_kernel, tmx=tmx),
        grid_spec=grid_spec,
        out_shape=jax.ShapeDtypeStruct((n_tiles * tmx, d), F32),
        compiler_params=_params(("arbitrary",)),
        name="moe_experts",
    )(tile_expert, tile_valid, src3, src3, src3, hf, w1, w3, w2)


COMBINE_SLOTS = 3


def _combine_kernel(p1a_ref, p2a_ref, p1b_ref, p2b_ref, p1c_ref, p2c_ref, hf_ref, rec_ref, o_hbm, g_ref, b_ref,
                    out0_ref, out1_ref, obuf, sems, *, alpha, tmc, split_tile):
    j = pl.program_id(0)
    n = pl.num_programs(0)

    def gather(p1_ref, p2_ref, s):
        _row_gather(p1_ref, o_hbm, obuf.at[s, 0], sems.at[s], tmc)
        _row_gather(p2_ref, o_hbm, obuf.at[s, 1], sems.at[s], tmc)

    def wait(s):
        _rows_wait(o_hbm, obuf.at[s, 0], sems.at[s], tmc)
        _rows_wait(o_hbm, obuf.at[s, 1], sems.at[s], tmc)

    @pl.when(j == 0)
    def _():
        gather(p1a_ref, p2a_ref, 0)
        gather(p1b_ref, p2b_ref, 1)
        wait(0)

    def step(write):
        slot = j % COMBINE_SLOTS
        rec = rec_ref[...]
        g1 = rec[:, R_G1:R_G1 + 1]
        g2 = rec[:, R_G2:R_G2 + 1]
        write(_layer_norm(alpha * hf_ref[...] + g1 * obuf[slot, 0] + g2 * obuf[slot, 1], g_ref[...], b_ref[...]))
        gather(p1c_ref, p2c_ref, (j + 2) % COMBINE_SLOTS)
        wait((j + 1) % COMBINE_SLOTS)

    def write_streams(y):
        out0_ref[...] = y
        out1_ref[...] = y.astype(BF16)

    def write_first(y):
        out0_ref[...] = y

    def write_second(y):
        out1_ref[...] = y

    if split_tile is None:
        step(write_streams)
    else:
        pl.when(j < split_tile)(lambda: step(write_first))
        pl.when(j >= split_tile)(lambda: step(write_second))

    @pl.when(j == n - 1)
    def _():
        wait((j + 2) % COMBINE_SLOTS)


def _combine(hf, rec, o_sorted, pos1, pos2, ln_g, ln_b, alpha, split_rows=None):
    t, d = hf.shape
    tmc = _tile(t if split_rows is None else math.gcd(t, split_rows), COMBINE_TILE)
    n_tiles = t // tmc
    p1 = pos1.reshape(n_tiles, 1, tmc)
    p2 = pos2.reshape(n_tiles, 1, tmc)
    def ahead(k):
        return pl.BlockSpec((1, 1, tmc), lambda j: (jnp.minimum(j + k, n_tiles - 1), 0, 0), memory_space=pltpu.SMEM)

    row = lambda w: pl.BlockSpec((tmc, w), lambda j: (j, 0))
    if split_rows is None:
        split_tile = None
        out_specs = [row(d), row(d)]
        out_shape = [jax.ShapeDtypeStruct((t, d), F32), jax.ShapeDtypeStruct((t, d), BF16)]
    else:
        split_tile = split_rows // tmc
        out_specs = [pl.BlockSpec((tmc, d), lambda j: (jnp.minimum(j, split_tile - 1), 0)),
                     pl.BlockSpec((tmc, d), lambda j: (jnp.maximum(j - split_tile, 0), 0))]
        out_shape = [jax.ShapeDtypeStruct((split_rows, d), F32), jax.ShapeDtypeStruct((t - split_rows, d), F32)]
    return pl.pallas_call(
        functools.partial(_combine_kernel, alpha=alpha, tmc=tmc, split_tile=split_tile),
        grid=(n_tiles,),
        in_specs=[ahead(0), ahead(0), ahead(1), ahead(1), ahead(2), ahead(2), row(d), row(LANES),
                  pl.BlockSpec(memory_space=pl.ANY),
                  pl.BlockSpec((1, d), lambda j: (0, 0)), pl.BlockSpec((1, d), lambda j: (0, 0))],
        out_specs=out_specs,
        out_shape=out_shape,
        scratch_shapes=[pltpu.VMEM((COMBINE_SLOTS, 2, tmc, d), F32), pltpu.SemaphoreType.DMA((COMBINE_SLOTS,))],
        compiler_params=_params(("arbitrary",)),
        name="moe_combine",
    )(p1, p2, p1, p2, p1, p2, hf, rec, o_sorted, ln_g.reshape(1, d), ln_b.reshape(1, d))


def _pad_to(a, axis, mult):
    pad = (-a.shape[axis]) % mult
    if pad == 0:
        return a
    widths = [(0, 0)] * a.ndim
    widths[axis] = (0, pad)
    return jnp.pad(a, widths)


def kernel(x_prompt, x_sample, ln_emb_g, ln_emb_b, w_in, sink, lambda_q1, lambda_k1, lambda_q2, lambda_k2,
           subln_g, w_pa, w_pd, w_o, ln1_g, ln1_b, ln2_g, ln2_b, ffn_w1, ffn_w3, ffn_w2, router_w,
           moe_w1, moe_w3, moe_w2):
    bp, seq, d = x_prompt.shape
    bs, seq_s, _ = x_sample.shape
    assert seq == seq_s
    depth = w_in.shape[0]
    n_seq = bp + bs
    t = n_seq * seq
    alpha = (2.0 * depth) ** 0.25
    slopes = [2.0 ** (-8.0 * (h + 1) / N_ALIBI_HEADS) for h in range(N_ALIBI_HEADS)]
    win_slopes2 = jnp.asarray([s * LOG2E for s in slopes[:WIN_Q_HEADS]], F32)
    diff_slopes2 = jnp.asarray([s * LOG2E for s in slopes[WIN_Q_HEADS:]], F32)
    col = jnp.arange(QKV_COLS)
    is_q = (col < WIN_Q) | ((col >= DIFF_Q_COL0) & (col < DIFF_Q_COL0 + DIFF_QK))
    col_scale = jnp.where(is_q, Q_SCALE, 1.0).astype(F32).reshape(1, QKV_COLS)
    tm_route = _tile(t, MERGE_ROW_TILE)

    w_qkv_b = w_in[:, :, :QKV_COLS].astype(BF16)
    w_g_b = w_in[:, :, QKV_COLS:].astype(BF16)
    w_pa_b, w_pd_b, w_o_b = (w.astype(BF16) for w in (w_pa, w_pd, w_o))
    ffn_w1_b = _pad_to(ffn_w1.astype(BF16), 2, 512)
    ffn_w3_b = _pad_to(ffn_w3.astype(BF16), 2, 512)
    ffn_w2_b = _pad_to(ffn_w2.astype(BF16), 1, 512)
    moe_w1_b, moe_w3_b, moe_w2_b = (w.astype(BF16) for w in (moe_w1, moe_w3, moe_w2))

    xf, xb = _embed_ln(x_prompt.reshape(bp * seq, d), x_sample.reshape(bs * seq, d), ln_emb_g, ln_emb_b)

    outs = None
    for l in range(depth):
        lambda_init = 0.8 - 0.6 * math.exp(-0.3 * l)
        z = _inproj(xb, w_qkv_b, l, col_scale)
        ya = _win_attn(z, jnp.concatenate([sink[l].astype(F32) * LOG2E, win_slopes2]), n_seq, seq)
        yd = _diff_attn(z, diff_slopes2, lambda_q1, lambda_k1, lambda_q2, lambda_k2, subln_g,
                        n_seq, seq, l, lambda_init)
        merge_args = (xb, ya, yd, xf, w_g_b, w_pa_b, w_pd_b, w_o_b, l, ln1_g[l], ln1_b[l], alpha)
        i = l // 2
        last = l == depth - 1
        if l % 2 == 0:
            hf, hb = _merge(*merge_args)
            xf, xb = _ffn(hb, hf, ffn_w1_b, ffn_w3_b, ffn_w2_b, i, ln2_g[l], ln2_b[l], alpha)
        else:
            rw = _pad_to(router_w[i].astype(F32), 1, LANES)
            r_hi = rw.astype(BF16)
            r_lo = (rw - r_hi.astype(F32)).astype(BF16)
            r_hl = r_hi + jnp.roll(r_lo, N_EXPERTS, axis=1)
            hf, rec, cnt = _merge(*merge_args, router=(r_hi, r_hl))
            pos1, pos2, src, tile_expert, tile_valid = _moe_plan(rec, cnt, tm_route, ROW_TILE)
            o_sorted = _experts(hf, src, tile_expert, tile_valid, moe_w1_b, moe_w3_b, moe_w2_b, i, ROW_TILE)
            res = _combine(hf, rec, o_sorted, pos1, pos2, ln2_g[l], ln2_b[l], alpha,
                           split_rows=bp * seq if last else None)
            if last:
                outs = res
            else:
                xf, xb = res

    if outs is None:
        outs = (xf[:bp * seq], xf[bp * seq:])
    return (outs[0].reshape(bp, seq, d), outs[1].reshape(bs, seq, d))
```

```python
import functools
import math

import jax
import jax.numpy as jnp
from jax import lax
from jax.experimental import pallas as pl
from jax.experimental.pallas import tpu as pltpu

HEAD_DIM = 128
WIN_Q_HEADS = 8
WIN_KV_HEADS = 2
WIN_REP = WIN_Q_HEADS // WIN_KV_HEADS
WINDOW = 128
BLOCK = 128
DIFF_HEADS = 4
N_ALIBI_HEADS = WIN_Q_HEADS + DIFF_HEADS
WIN_Q = WIN_Q_HEADS * HEAD_DIM
WIN_KV = WIN_KV_HEADS * HEAD_DIM
DIFF_QK = DIFF_HEADS * 2 * HEAD_DIM
DIFF_V = DIFF_HEADS * 2 * HEAD_DIM
QKV_COLS = WIN_Q + 2 * WIN_KV + 2 * DIFF_QK + DIFF_V
DIFF_Q_COL0 = WIN_Q + 2 * WIN_KV
N_EXPERTS = 8
LN_EPS = 1e-5
NEG_INF = -1e30
LOG2E = 1.4426950408889634
Q_SCALE = (HEAD_DIM ** -0.5) * LOG2E
LANES = 128
R_E1, R_E2, R_G1, R_G2, R_RANK1, R_RANK2 = range(6)

VMEM_LIMIT = 56 * 1024 * 1024
F32 = jnp.float32
BF16 = jnp.bfloat16
I32 = jnp.int32

ROW_TILE = 512
MERGE_ROW_TILE = 256
FFN_ROW_TILE = 768
COMBINE_TILE = 256


def _params(sem, vmem=VMEM_LIMIT):
    return pltpu.CompilerParams(dimension_semantics=sem, vmem_limit_bytes=vmem)


def _tile(n, want):
    for t in range(min(n, want), 0, -1):
        if n % t == 0 and (t % 8 == 0 or t == n):
            return t
    raise ValueError((n, want))


def _layer_norm(v, g, b):
    mu = jnp.mean(v, axis=-1, keepdims=True)
    xc = v - mu
    var = jnp.mean(xc * xc, axis=-1, keepdims=True)
    return xc * lax.rsqrt(var + LN_EPS) * g + b


def _dot(a, b):
    return jnp.dot(a, b, preferred_element_type=F32)


def _dot_t(a, b):
    return lax.dot_general(a, b, (((1,), (1,)), ((), ())), preferred_element_type=F32)


def _embed_ln_kernel(xp_ref, xs_ref, g_ref, b_ref, of_ref, ob_ref, *, n_prompt_tiles):
    i = pl.program_id(0)

    def emit(x_ref):
        y = _layer_norm(x_ref[...], g_ref[...], b_ref[...])
        of_ref[...] = y
        ob_ref[...] = y.astype(BF16)

    @pl.when(i < n_prompt_tiles)
    def _():
        emit(xp_ref)

    @pl.when(i >= n_prompt_tiles)
    def _():
        emit(xs_ref)


def _embed_ln(xp, xs, g, b):
    tp, d = xp.shape
    ts = xs.shape[0]
    tm = _tile(math.gcd(tp, ts), 512)
    npt, nst = tp // tm, ts // tm
    t = tp + ts
    return pl.pallas_call(
        functools.partial(_embed_ln_kernel, n_prompt_tiles=npt),
        grid=(npt + nst,),
        in_specs=[
            pl.BlockSpec((tm, d), lambda i: (jnp.minimum(i, npt - 1), 0)),
            pl.BlockSpec((tm, d), lambda i: (jnp.maximum(i - npt, 0), 0)),
            pl.BlockSpec((1, d), lambda i: (0, 0)),
            pl.BlockSpec((1, d), lambda i: (0, 0)),
        ],
        out_specs=[
            pl.BlockSpec((tm, d), lambda i: (i, 0)),
            pl.BlockSpec((tm, d), lambda i: (i, 0)),
        ],
        out_shape=[jax.ShapeDtypeStruct((t, d), F32), jax.ShapeDtypeStruct((t, d), BF16)],
        compiler_params=_params(("arbitrary",)),
        name="embed_ln",
    )(xp, xs, g.reshape(1, d), b.reshape(1, d))


def _inproj_kernel(x_ref, w_ref, s_ref, o_ref):
    o_ref[...] = (_dot(x_ref[...], w_ref[...]) * s_ref[...]).astype(o_ref.dtype)


def _inproj(xb, w_in, layer, col_scale):
    t, d = xb.shape
    n = QKV_COLS
    tm = _tile(t, 1024)
    tn = _tile(n, 1536)
    return pl.pallas_call(
        _inproj_kernel,
        grid=(t // tm, n // tn),
        in_specs=[
            pl.BlockSpec((tm, d), lambda i, j: (i, 0)),
            pl.BlockSpec((None, d, tn), lambda i, j: (layer, 0, j)),
            pl.BlockSpec((1, tn), lambda i, j: (0, j)),
        ],
        out_specs=pl.BlockSpec((tm, tn), lambda i, j: (i, j)),
        out_shape=jax.ShapeDtypeStruct((t, n), BF16),
        compiler_params=_params(("parallel", "arbitrary")),
        name="inproj",
    )(xb, w_in, col_scale)


def _lane_tiles(x):
    return [x[:, u * LANES:(u + 1) * LANES] for u in range(x.shape[1] // LANES)]


def _win_attn_kernel(sc_ref, q_ref, k_ref, v_ref, o_ref, t_ref, *, seq):
    g = pl.program_id(1)
    kw = min(3 * BLOCK, seq)
    nb = seq // BLOCK
    row = lax.broadcasted_iota(I32, (BLOCK, kw), 0)
    col = lax.broadcasted_iota(I32, (BLOCK, kw), 1)
    k_starts = [min(max(n * BLOCK - BLOCK, 0), seq - kw) for n in range(nb)]
    offsets = sorted({n * BLOCK - k_starts[n] for n in range(nb)})
    dists = {off: jnp.abs(row - col + off) for off in offsets}
    head_cols = [slice(r * HEAD_DIM, (r + 1) * HEAD_DIM) for r in range(WIN_REP)]
    ms = {}
    for r, cs in enumerate(head_cols):
        sink2 = sc_ref[g * WIN_REP + r]
        slope2 = sc_ref[WIN_Q_HEADS + g * WIN_REP + r]
        bias = {off: jnp.where(d <= WINDOW, slope2 * d.astype(F32), -NEG_INF) for off, d in dists.items()}
        for n in range(nb):
            qs = slice(n * BLOCK, (n + 1) * BLOCK)
            ks = slice(k_starts[n], k_starts[n] + kw)
            t = _dot_t(q_ref[qs, cs], k_ref[ks, :]) - bias[n * BLOCK - k_starts[n]]
            t_ref[r, qs, :] = t
            mx = functools.reduce(jnp.maximum, _lane_tiles(t))
            ms[r, n] = jnp.maximum(jnp.max(mx, axis=-1, keepdims=True), sink2)
    for r, cs in enumerate(head_cols):
        sink2 = sc_ref[g * WIN_REP + r]
        for n in range(nb):
            qs = slice(n * BLOCK, (n + 1) * BLOCK)
            ks = slice(k_starts[n], k_starts[n] + kw)
            m = ms[r, n]
            p = jnp.exp2(t_ref[r, qs, :] - m)
            ls = functools.reduce(jnp.add, _lane_tiles(p))
            denom = jnp.sum(ls, axis=-1, keepdims=True) + jnp.exp2(sink2 - m)
            o_ref[qs, cs] = (_dot(p.astype(BF16), v_ref[ks, :]) / denom).astype(o_ref.dtype)


def _win_attn(z, scalars, n_seq, seq):
    gq = WIN_REP * HEAD_DIM
    kw = min(3 * BLOCK, seq)
    k_blk0 = WIN_Q // HEAD_DIM
    v_blk0 = (WIN_Q + WIN_KV) // HEAD_DIM
    return pl.pallas_call(
        functools.partial(_win_attn_kernel, seq=seq),
        grid=(n_seq, WIN_KV_HEADS),
        in_specs=[
            pl.BlockSpec(memory_space=pltpu.SMEM),
            pl.BlockSpec((seq, gq), lambda b, g: (b, g)),
            pl.BlockSpec((seq, HEAD_DIM), lambda b, g: (b, k_blk0 + g)),
            pl.BlockSpec((seq, HEAD_DIM), lambda b, g: (b, v_blk0 + g)),
        ],
        out_specs=pl.BlockSpec((seq, gq), lambda b, g: (b, g)),
        out_shape=jax.ShapeDtypeStruct((n_seq * seq, WIN_Q), BF16),
        scratch_shapes=[pltpu.VMEM((WIN_REP, seq, kw), F32)],
        compiler_params=_params(("parallel", "arbitrary")),
        name="win_attn",
    )(scalars, z, z, z)


def _diff_attn_kernel(sl_ref, lq1_ref, lk1_ref, lq2_ref, lk2_ref, g_ref, q_ref, k_ref, v_ref, o_ref,
                      bias_ref, t_ref, *, seq, tq, rb_rows, kc, layer, lambda_init):
    h = pl.program_id(0)
    qi = pl.program_id(1)
    b = pl.program_id(2)
    n_kc = seq // kc

    @pl.when(b == 0)
    def _():
        tpos = qi * tq + lax.broadcasted_iota(I32, (tq, seq), 0)
        spos = lax.broadcasted_iota(I32, (tq, seq), 1)
        bias_ref[...] = sl_ref[h] * jnp.abs(tpos - spos).astype(F32)

    lam = (jnp.exp(jnp.sum(lq1_ref[layer:layer + 1, :] * lk1_ref[layer:layer + 1, :], axis=-1, keepdims=True))
           - jnp.exp(jnp.sum(lq2_ref[layer:layer + 1, :] * lk2_ref[layer:layer + 1, :], axis=-1, keepdims=True))
           + lambda_init)

    sections = [(rb, c) for rb in range(tq // rb_rows) for c in range(2)]
    key_tiles = [slice(j * kc, (j + 1) * kc) for j in range(n_kc)]
    ms = {}
    for rb, c in sections:
        rs = slice(rb * rb_rows, (rb + 1) * rb_rows)
        cs = slice(c * HEAD_DIM, (c + 1) * HEAD_DIM)
        q = q_ref[rs, cs]
        mx = jnp.full((rb_rows, LANES), -jnp.inf, F32)
        for ks in key_tiles:
            t = _dot_t(q, k_ref[ks, cs]) - bias_ref[rs, ks]
            t_ref[c, rs, ks] = t
            mx = functools.reduce(jnp.maximum, _lane_tiles(t), mx)
        ms[rb, c] = jnp.max(mx, axis=-1, keepdims=True)
    outs = {}
    for rb, c in sections:
        rs = slice(rb * rb_rows, (rb + 1) * rb_rows)
        ls = jnp.zeros((rb_rows, LANES), F32)
        acc = jnp.zeros((rb_rows, 2 * HEAD_DIM), F32)
        for ks in key_tiles:
            p = jnp.exp2(t_ref[c, rs, ks] - ms[rb, c])
            ls = functools.reduce(jnp.add, _lane_tiles(p), ls)
            acc = acc + _dot(p.astype(BF16), v_ref[ks, :])
        outs[rb, c] = acc / jnp.sum(ls, axis=-1, keepdims=True)
    for rb in range(tq // rb_rows):
        rs = slice(rb * rb_rows, (rb + 1) * rb_rows)
        o = outs[rb, 0] - lam * outs[rb, 1]
        rms = lax.rsqrt(jnp.mean(o * o, axis=-1, keepdims=True) + LN_EPS)
        o_ref[rs, :] = (o * rms * g_ref[layer:layer + 1, :] * (1.0 - lambda_init)).astype(o_ref.dtype)


def _diff_attn(z, slopes2, lq1, lk1, lq2, lk2, subln_g, n_seq, seq, layer, lambda_init):
    hw = 2 * HEAD_DIM
    tq = _tile(seq, 1024)
    rb_rows = _tile(tq, 128)
    kc = _tile(seq, 256)
    nq = seq // tq
    q_blk0 = DIFF_Q_COL0 // hw
    k_blk0 = q_blk0 + DIFF_QK // hw
    v_blk0 = k_blk0 + DIFF_QK // hw
    whole = lambda a: pl.BlockSpec(a.shape, lambda h, qi, b: (0, 0))
    return pl.pallas_call(
        functools.partial(_diff_attn_kernel, seq=seq, tq=tq, rb_rows=rb_rows, kc=kc, layer=layer,
                          lambda_init=lambda_init),
        grid=(DIFF_HEADS, nq, n_seq),
        in_specs=[
            pl.BlockSpec(memory_space=pltpu.SMEM),
            whole(lq1), whole(lk1), whole(lq2), whole(lk2), whole(subln_g),
            pl.BlockSpec((tq, hw), lambda h, qi, b: (b * nq + qi, q_blk0 + h)),
            pl.BlockSpec((seq, hw), lambda h, qi, b: (b, k_blk0 + h)),
            pl.BlockSpec((seq, hw), lambda h, qi, b: (b, v_blk0 + h)),
        ],
        out_specs=pl.BlockSpec((tq, hw), lambda h, qi, b: (b * nq + qi, h)),
        out_shape=jax.ShapeDtypeStruct((n_seq * seq, DIFF_V), BF16),
        scratch_shapes=[pltpu.VMEM((tq, seq), F32), pltpu.VMEM((2, tq, seq), F32)],
        compiler_params=_params(("arbitrary", "arbitrary", "arbitrary")),
        name="diff_attn",
    )(slopes2, lq1, lk1, lq2, lk2, subln_g, z, z, z)


def _residual_copy(x_hbm, buf, sem, tm):
    row0 = pl.multiple_of(pl.program_id(0) * tm, tm)
    return pltpu.make_async_copy(x_hbm.at[pl.ds(row0, tm), :], buf, sem)


def _route(hf, rhi_ref, rhl_ref):
    tm = hf.shape[0]
    h_hi = hf.astype(BF16)
    h_lo = (hf - h_hi.astype(F32)).astype(BF16)
    hh = _dot(h_hi, rhl_ref[...])
    logits = hh + pltpu.roll(hh, LANES - N_EXPERTS, 1) + _dot(h_lo, rhi_ref[...])
    lane = lax.broadcasted_iota(I32, logits.shape, 1)
    lg = jnp.where(lane < N_EXPERTS, logits, -jnp.inf)
    m1 = jnp.max(lg, axis=-1, keepdims=True)
    i1 = jnp.min(jnp.where(lg == m1, lane, LANES), axis=-1, keepdims=True)
    lg2 = jnp.where(lane == i1, -jnp.inf, lg)
    m2 = jnp.max(lg2, axis=-1, keepdims=True)
    i2 = jnp.min(jnp.where(lg2 == m2, lane, LANES), axis=-1, keepdims=True)
    e2 = jnp.exp(m2 - m1)
    g1 = 1.0 / (1.0 + e2)
    g2 = e2 / (1.0 + e2)
    oh1 = lane == i1
    oh2 = lane == i2
    sel = jnp.logical_or(oh1, oh2).astype(F32)
    r_idx = lax.broadcasted_iota(I32, (tm, tm), 0)
    c_idx = lax.broadcasted_iota(I32, (tm, tm), 1)
    lower = (c_idx < r_idx).astype(F32).astype(BF16)
    before = _dot(lower, sel.astype(BF16))
    rank1 = jnp.sum(jnp.where(oh1, before, 0.0), axis=-1, keepdims=True)
    rank2 = jnp.sum(jnp.where(oh2, before, 0.0), axis=-1, keepdims=True)
    counts = jnp.sum(sel, axis=0, keepdims=True)
    rec = jnp.zeros(logits.shape, F32)
    for ln, val in ((R_E1, i1.astype(F32)), (R_E2, i2.astype(F32)), (R_G1, g1), (R_G2, g2),
                    (R_RANK1, rank1), (R_RANK2, rank2)):
        rec = jnp.where(lane == ln, val, rec)
    return rec, counts


def _merge_kernel(*refs, alpha, with_router, d, tn):
    if with_router:
        (xb_ref, ya_ref, yd_ref, xf_ref, wg_ref, wpa_ref, wpd_ref, wo_ref, g_ref, b_ref,
         rhi_ref, rhl_ref, hf_ref, rec_ref, cnt_ref) = refs
    else:
        (xb_ref, ya_ref, yd_ref, xf_ref, wg_ref, wpa_ref, wpd_ref, wo_ref, g_ref, b_ref, hf_ref, hb_ref) = refs
    x = xb_ref[...]
    ya = ya_ref[...]
    yd = yd_ref[...]
    mix = None
    for j in range(d // tn):
        cs = slice(j * tn, (j + 1) * tn)
        ga = _dot(x, wg_ref[:, cs])
        gd = _dot(x, wg_ref[:, d + j * tn:d + (j + 1) * tn])
        pa = _dot(ya, wpa_ref[:, cs])
        pd = _dot(yd, wpd_ref[:, cs])
        merged = (jax.nn.sigmoid(ga) * pa + jax.nn.sigmoid(gd) * pd).astype(BF16)
        part = _dot(merged, wo_ref[cs, :])
        mix = part if mix is None else mix + part
    hf = _layer_norm(alpha * xf_ref[...] + mix, g_ref[...], b_ref[...])
    hf_ref[...] = hf
    if with_router:
        rec, counts = _route(hf, rhi_ref, rhl_ref)
        rec_ref[...] = rec
        cnt_ref[...] = jnp.broadcast_to(counts, cnt_ref.shape)
    else:
        hb_ref[...] = hf.astype(BF16)


def _merge(xb, ya, yd, xf, w_g, w_pa, w_pd, w_o, layer, ln_g, ln_b, alpha, router=None):
    t, d = xb.shape
    tm = _tile(t, MERGE_ROW_TILE)
    tn = _tile(d, 512)
    with_router = router is not None
    row = lambda w: pl.BlockSpec((tm, w), lambda i: (i, 0))

    def resident(a):
        return pl.BlockSpec((None,) + a.shape[1:], lambda i: (layer, 0, 0), pipeline_mode=pl.Buffered(1))

    in_specs = [
        row(d), row(WIN_Q), row(DIFF_V), row(d),
        resident(w_g), resident(w_pa), resident(w_pd), resident(w_o),
        pl.BlockSpec((1, d), lambda i: (0, 0)),
        pl.BlockSpec((1, d), lambda i: (0, 0)),
    ]
    args = [xb, ya, yd, xf, w_g, w_pa, w_pd, w_o, ln_g.reshape(1, d), ln_b.reshape(1, d)]
    out_specs = [row(d)]
    out_shape = [jax.ShapeDtypeStruct((t, d), F32)]
    if with_router:
        in_specs += [pl.BlockSpec((d, LANES), lambda i: (0, 0))] * 2
        args += list(router)
        out_specs += [row(LANES), pl.BlockSpec((8, LANES), lambda i: (i, 0))]
        out_shape += [jax.ShapeDtypeStruct((t, LANES), F32), jax.ShapeDtypeStruct((t // tm * 8, LANES), F32)]
    else:
        out_specs.append(row(d))
        out_shape.append(jax.ShapeDtypeStruct((t, d), BF16))
    return pl.pallas_call(
        functools.partial(_merge_kernel, alpha=alpha, with_router=with_router, d=d, tn=tn),
        grid=(t // tm,),
        in_specs=in_specs,
        out_specs=out_specs,
        out_shape=out_shape,
        compiler_params=_params(("arbitrary",)),
        name="merge_router" if with_router else "merge",
    )(*args)


def _swiglu(x, w1, w3):
    u = _dot(x, w1)
    return (u * jax.nn.sigmoid(u) * _dot(x, w3)).astype(BF16)


def _ffn_kernel(hb_ref, hf_hbm, w1_ref, w3_ref, w2_ref, g_ref, b_ref, of_ref, ob_ref, hf_buf, hf_sem, *, alpha, tm):
    f = pl.program_id(1)

    @pl.when(f == 0)
    def _():
        _residual_copy(hf_hbm, hf_buf, hf_sem, tm).start()
        of_ref[...] = jnp.zeros(of_ref.shape, F32)

    of_ref[...] += _dot(_swiglu(hb_ref[...], w1_ref[...], w3_ref[...]), w2_ref[...])

    @pl.when(f == pl.num_programs(1) - 1)
    def _():
        _residual_copy(hf_hbm, hf_buf, hf_sem, tm).wait()
        y = _layer_norm(alpha * hf_buf[...] + of_ref[...], g_ref[...], b_ref[...])
        of_ref[...] = y
        ob_ref[...] = y.astype(BF16)


def _ffn(hb, hf, w1, w3, w2, idx, ln_g, ln_b, alpha):
    t, d = hb.shape
    ff = w1.shape[2]
    tm = _tile(t, FFN_ROW_TILE)
    tf = _tile(ff, 512)
    row = pl.BlockSpec((tm, d), lambda i, f: (i, 0))
    return pl.pallas_call(
        functools.partial(_ffn_kernel, alpha=alpha, tm=tm),
        grid=(t // tm, ff // tf),
        in_specs=[
            row,
            pl.BlockSpec(memory_space=pl.ANY),
            pl.BlockSpec((None, d, tf), lambda i, f: (idx, 0, f)),
            pl.BlockSpec((None, d, tf), lambda i, f: (idx, 0, f)),
            pl.BlockSpec((None, tf, d), lambda i, f: (idx, f, 0)),
            pl.BlockSpec((1, d), lambda i, f: (0, 0)),
            pl.BlockSpec((1, d), lambda i, f: (0, 0)),
        ],
        out_specs=[row, row],
        out_shape=[jax.ShapeDtypeStruct((t, d), F32), jax.ShapeDtypeStruct((t, d), BF16)],
        scratch_shapes=[pltpu.VMEM((tm, d), F32), pltpu.SemaphoreType.DMA(())],
        compiler_params=_params(("arbitrary", "arbitrary")),
        name="ffn",
    )(hb, hf, w1, w3, w2, ln_g.reshape(1, d), ln_b.reshape(1, d))


def _moe_plan(rec, cnt, tm_route, tmx):
    t = rec.shape[0]
    e1 = rec[:, R_E1].astype(I32)
    e2 = rec[:, R_E2].astype(I32)
    rank1 = rec[:, R_RANK1].astype(I32)
    rank2 = rec[:, R_RANK2].astype(I32)
    tile_cnt = cnt.reshape(t // tm_route, 8, LANES)[:, 0, :N_EXPERTS].astype(I32)
    counts = jnp.sum(tile_cnt, axis=0)
    tile_base = jnp.cumsum(tile_cnt, axis=0) - tile_cnt
    pad_cnt = (counts + tmx - 1) // tmx * tmx
    pad_end = jnp.cumsum(pad_cnt)
    pad_start = pad_end - pad_cnt
    base = jnp.repeat(tile_base + pad_start[None, :], tm_route, axis=0)
    eids = jnp.arange(N_EXPERTS, dtype=I32)[None, :]
    pos1 = jnp.sum(jnp.where(e1[:, None] == eids, base, 0), axis=-1) + rank1
    pos2 = jnp.sum(jnp.where(e2[:, None] == eids, base, 0), axis=-1) + rank2
    n_rows = 2 * t + N_EXPERTS * tmx
    jj = jnp.arange(tmx, dtype=I32)[None, :]
    dummy_pos = jnp.where(jj < (pad_cnt - counts)[:, None], (pad_start + counts)[:, None] + jj, n_rows)
    tok = jnp.arange(t, dtype=I32)
    keys = jnp.concatenate([pos1, pos2, dummy_pos.reshape(-1)])
    vals = jnp.concatenate([tok, tok, jnp.zeros((N_EXPERTS * tmx,), I32)])
    _, src = lax.sort((keys, vals), num_keys=1)
    n_tiles = n_rows // tmx
    tile_start = jnp.arange(n_tiles, dtype=I32) * tmx
    tile_expert = jnp.minimum(jnp.sum((tile_start[:, None] >= pad_end[None, :]).astype(I32), axis=-1), N_EXPERTS - 1)
    tile_valid = (tile_start < pad_end[-1]).astype(I32)
    return pos1, pos2, src, tile_expert, tile_valid


def _row_gather(idx_ref, src_hbm, dst, sem, n, both_queues=False):
    for r in range(n):
        pltpu.make_async_copy(src_hbm.at[pl.ds(idx_ref[0, 0, r], 1), :], dst.at[pl.ds(r, 1), :], sem).start(
            priority=r % 2 if both_queues else 0)


def _rows_wait(src_hbm, dst, sem, n):
    pltpu.make_async_copy(src_hbm.at[pl.ds(0, n), :], dst, sem).wait()


GATHER_SLOTS = 3


def _experts_kernel(te_ref, tv_ref, ia_ref, ib_ref, ic_ref, x_hbm, w1_ref, w3_ref, w2_ref, o_ref, xbuf, xb_ref, sems,
                    *, tmx):
    j = pl.program_id(0)
    n = pl.num_programs(0)
    slot = j % GATHER_SLOTS
    valid = tv_ref[j] > 0

    def advance():
        nxt2 = (j + 2) % GATHER_SLOTS
        nxt1 = (j + 1) % GATHER_SLOTS
        _row_gather(ic_ref, x_hbm, xbuf.at[nxt2], sems.at[nxt2], tmx)
        _rows_wait(x_hbm, xbuf.at[nxt1], sems.at[nxt1], tmx)

    @pl.when(j == 0)
    def _():
        _row_gather(ia_ref, x_hbm, xbuf.at[0], sems.at[0], tmx)
        _row_gather(ib_ref, x_hbm, xbuf.at[1], sems.at[1], tmx)
        _rows_wait(x_hbm, xbuf.at[0], sems.at[0], tmx)

    @pl.when(valid)
    def _():
        xb_ref[...] = xbuf[slot].astype(BF16)
        o_ref[...] = _dot(_swiglu(xb_ref[...], w1_ref[...], w3_ref[...]), w2_ref[...])
        advance()

    @pl.when(jnp.logical_not(valid))
    def _():
        o_ref[...] = jnp.zeros(o_ref.shape, F32)
        advance()

    @pl.when(j == n - 1)
    def _():
        last = (j + 2) % GATHER_SLOTS
        _rows_wait(x_hbm, xbuf.at[last], sems.at[last], tmx)


def _experts(hf, src, tile_expert, tile_valid, w1, w3, w2, idx, tmx):
    t, d = hf.shape
    ff = w1.shape[3]
    n_tiles = src.shape[0] // tmx
    src3 = src.reshape(n_tiles, 1, tmx)

    def ahead(k):
        return pl.BlockSpec((1, 1, tmx), lambda j, te, tv: (jnp.minimum(j + k, n_tiles - 1), 0, 0),
                            memory_space=pltpu.SMEM)

    grid_spec = pltpu.PrefetchScalarGridSpec(
        num_scalar_prefetch=2,
        grid=(n_tiles,),
        in_specs=[
            ahead(0), ahead(1), ahead(2),
            pl.BlockSpec(memory_space=pl.ANY),
            pl.BlockSpec((None, None, d, ff), lambda j, te, tv: (idx, te[j], 0, 0)),
            pl.BlockSpec((None, None, d, ff), lambda j, te, tv: (idx, te[j], 0, 0)),
            pl.BlockSpec((None, None, ff, d), lambda j, te, tv: (idx, te[j], 0, 0)),
        ],
        out_specs=pl.BlockSpec((tmx, d), lambda j, te, tv: (j, 0)),
        scratch_shapes=[pltpu.VMEM((GATHER_SLOTS, tmx, d), F32), pltpu.VMEM((tmx, d), BF16),
                        pltpu.SemaphoreType.DMA((GATHER_SLOTS,))],
    )
    return pl.pallas_call(
        functools.partial(_experts_kernel, tmx=tmx),
        grid_spec=grid_spec,
        out_shape=jax.ShapeDtypeStruct((n_tiles * tmx, d), F32),
        compiler_params=_params(("arbitrary",)),
        name="moe_experts",
    )(tile_expert, tile_valid, src3, src3, src3, hf, w1, w3, w2)


COMBINE_SLOTS = 3


def _combine_kernel(p1a_ref, p2a_ref, p1b_ref, p2b_ref, p1c_ref, p2c_ref, hf_ref, rec_ref, o_hbm, g_ref, b_ref,
                    out0_ref, out1_ref, obuf, sems, *, alpha, tmc, split_tile):
    j = pl.program_id(0)
    n = pl.num_programs(0)

    def gather(p1_ref, p2_ref, s):
        _row_gather(p1_ref, o_hbm, obuf.at[s, 0], sems.at[s], tmc, both_queues=True)
        _row_gather(p2_ref, o_hbm, obuf.at[s, 1], sems.at[s], tmc, both_queues=True)

    def wait(s):
        _rows_wait(o_hbm, obuf.at[s, 0], sems.at[s], tmc)
        _rows_wait(o_hbm, obuf.at[s, 1], sems.at[s], tmc)

    @pl.when(j == 0)
    def _():
        gather(p1a_ref, p2a_ref, 0)
        gather(p1b_ref, p2b_ref, 1)
        wait(0)

    def step(write):
        slot = j % COMBINE_SLOTS
        rec = rec_ref[...]
        g1 = rec[:, R_G1:R_G1 + 1]
        g2 = rec[:, R_G2:R_G2 + 1]
        write(_layer_norm(alpha * hf_ref[...] + g1 * obuf[slot, 0] + g2 * obuf[slot, 1], g_ref[...], b_ref[...]))
        gather(p1c_ref, p2c_ref, (j + 2) % COMBINE_SLOTS)
        wait((j + 1) % COMBINE_SLOTS)

    def write_streams(y):
        out0_ref[...] = y
        out1_ref[...] = y.astype(BF16)

    def write_first(y):
        out0_ref[...] = y

    def write_second(y):
        out1_ref[...] = y

    if split_tile is None:
        step(write_streams)
    else:
        pl.when(j < split_tile)(lambda: step(write_first))
        pl.when(j >= split_tile)(lambda: step(write_second))

    @pl.when(j == n - 1)
    def _():
        wait((j + 2) % COMBINE_SLOTS)


def _combine(hf, rec, o_sorted, pos1, pos2, ln_g, ln_b, alpha, split_rows=None):
    t, d = hf.shape
    tmc = _tile(t if split_rows is None else math.gcd(t, split_rows), COMBINE_TILE)
    n_tiles = t // tmc
    p1 = pos1.reshape(n_tiles, 1, tmc)
    p2 = pos2.reshape(n_tiles, 1, tmc)
    def ahead(k):
        return pl.BlockSpec((1, 1, tmc), lambda j: (jnp.minimum(j + k, n_tiles - 1), 0, 0), memory_space=pltpu.SMEM)

    row = lambda w: pl.BlockSpec((tmc, w), lambda j: (j, 0))
    if split_rows is None:
        split_tile = None
        out_specs = [row(d), row(d)]
        out_shape = [jax.ShapeDtypeStruct((t, d), F32), jax.ShapeDtypeStruct((t, d), BF16)]
    else:
        split_tile = split_rows // tmc
        out_specs = [pl.BlockSpec((tmc, d), lambda j: (jnp.minimum(j, split_tile - 1), 0)),
                     pl.BlockSpec((tmc, d), lambda j: (jnp.maximum(j - split_tile, 0), 0))]
        out_shape = [jax.ShapeDtypeStruct((split_rows, d), F32), jax.ShapeDtypeStruct((t - split_rows, d), F32)]
    return pl.pallas_call(
        functools.partial(_combine_kernel, alpha=alpha, tmc=tmc, split_tile=split_tile),
        grid=(n_tiles,),
        in_specs=[ahead(0), ahead(0), ahead(1), ahead(1), ahead(2), ahead(2), row(d), row(LANES),
                  pl.BlockSpec(memory_space=pl.ANY),
                  pl.BlockSpec((1, d), lambda j: (0, 0)), pl.BlockSpec((1, d), lambda j: (0, 0))],
        out_specs=out_specs,
        out_shape=out_shape,
        scratch_shapes=[pltpu.VMEM((COMBINE_SLOTS, 2, tmc, d), F32), pltpu.SemaphoreType.DMA((COMBINE_SLOTS,))],
        compiler_params=_params(("arbitrary",)),
        name="moe_combine",
    )(p1, p2, p1, p2, p1, p2, hf, rec, o_sorted, ln_g.reshape(1, d), ln_b.reshape(1, d))


def _pad_to(a, axis, mult):
    pad = (-a.shape[axis]) % mult
    if pad == 0:
        return a
    widths = [(0, 0)] * a.ndim
    widths[axis] = (0, pad)
    return jnp.pad(a, widths)


def kernel(x_prompt, x_sample, ln_emb_g, ln_emb_b, w_in, sink, lambda_q1, lambda_k1, lambda_q2, lambda_k2,
           subln_g, w_pa, w_pd, w_o, ln1_g, ln1_b, ln2_g, ln2_b, ffn_w1, ffn_w3, ffn_w2, router_w,
           moe_w1, moe_w3, moe_w2):
    bp, seq, d = x_prompt.shape
    bs, seq_s, _ = x_sample.shape
    assert seq == seq_s
    depth = w_in.shape[0]
    n_seq = bp + bs
    t = n_seq * seq
    alpha = (2.0 * depth) ** 0.25
    slopes = [2.0 ** (-8.0 * (h + 1) / N_ALIBI_HEADS) for h in range(N_ALIBI_HEADS)]
    win_slopes2 = jnp.asarray([s * LOG2E for s in slopes[:WIN_Q_HEADS]], F32)
    diff_slopes2 = jnp.asarray([s * LOG2E for s in slopes[WIN_Q_HEADS:]], F32)
    col = jnp.arange(QKV_COLS)
    is_q = (col < WIN_Q) | ((col >= DIFF_Q_COL0) & (col < DIFF_Q_COL0 + DIFF_QK))
    col_scale = jnp.where(is_q, Q_SCALE, 1.0).astype(F32).reshape(1, QKV_COLS)
    tm_route = _tile(t, MERGE_ROW_TILE)

    w_qkv_b = w_in[:, :, :QKV_COLS].astype(BF16)
    w_g_b = w_in[:, :, QKV_COLS:].astype(BF16)
    w_pa_b, w_pd_b, w_o_b = (w.astype(BF16) for w in (w_pa, w_pd, w_o))
    ffn_w1_b = _pad_to(ffn_w1.astype(BF16), 2, 512)
    ffn_w3_b = _pad_to(ffn_w3.astype(BF16), 2, 512)
    ffn_w2_b = _pad_to(ffn_w2.astype(BF16), 1, 512)
    moe_w1_b, moe_w3_b, moe_w2_b = (w.astype(BF16) for w in (moe_w1, moe_w3, moe_w2))

    xf, xb = _embed_ln(x_prompt.reshape(bp * seq, d), x_sample.reshape(bs * seq, d), ln_emb_g, ln_emb_b)

    outs = None
    for l in range(depth):
        lambda_init = 0.8 - 0.6 * math.exp(-0.3 * l)
        z = _inproj(xb, w_qkv_b, l, col_scale)
        ya = _win_attn(z, jnp.concatenate([sink[l].astype(F32) * LOG2E, win_slopes2]), n_seq, seq)
        yd = _diff_attn(z, diff_slopes2, lambda_q1, lambda_k1, lambda_q2, lambda_k2, subln_g,
                        n_seq, seq, l, lambda_init)
        merge_args = (xb, ya, yd, xf, w_g_b, w_pa_b, w_pd_b, w_o_b, l, ln1_g[l], ln1_b[l], alpha)
        i = l // 2
        last = l == depth - 1
        if l % 2 == 0:
            hf, hb = _merge(*merge_args)
            xf, xb = _ffn(hb, hf, ffn_w1_b, ffn_w3_b, ffn_w2_b, i, ln2_g[l], ln2_b[l], alpha)
        else:
            rw = _pad_to(router_w[i].astype(F32), 1, LANES)
            r_hi = rw.astype(BF16)
            r_lo = (rw - r_hi.astype(F32)).astype(BF16)
            r_hl = r_hi + jnp.roll(r_lo, N_EXPERTS, axis=1)
            hf, rec, cnt = _merge(*merge_args, router=(r_hi, r_hl))
            pos1, pos2, src, tile_expert, tile_valid = _moe_plan(rec, cnt, tm_route, ROW_TILE)
            o_sorted = _experts(hf, src, tile_expert, tile_valid, moe_w1_b, moe_w3_b, moe_w2_b, i, ROW_TILE)
            res = _combine(hf, rec, o_sorted, pos1, pos2, ln2_g[l], ln2_b[l], alpha,
                           split_rows=bp * seq if last else None)
            if last:
                outs = res
            else:
                xf, xb = res

    if outs is None:
        outs = (xf[:bp * seq], xf[bp * seq:])
    return (outs[0].reshape(bp, seq, d), outs[1].reshape(bs, seq, d))
```
